```python
import jax, jax.numpy as jnp
from jax import lax
import numpy as np

D_MODEL = 4096
BATCH = 4
SEQ = 4096
DEPTH = 2

EPS = 1e-6
BLOCK = 128

SB_HEADS = 8
SB_DIM = 128
ML_HEADS = 8
ML_DQK = 128
ML_DV = 128
ML_CONV = 4
ML_CHUNK = 128
SG_GROUPS = 8
SG_DIM = 128
SG_CHUNK = 128
SW_QHEADS = 16
SW_KVHEADS = 2
SW_DIM = 64
SW_WINDOW = 128

N_BRANCH = 4
BRANCH_W = 1024

A_W = SB_HEADS * SB_DIM
B_QK = ML_HEADS * ML_DQK
B_V = ML_HEADS * ML_DV
C_W = SG_GROUPS * SG_DIM
D_Q = SW_QHEADS * SW_DIM
D_KV = SW_KVHEADS * SW_DIM

SPLIT_SIZES = (A_W, A_W, A_W,
               2 * B_QK, B_V, B_V,
               ML_HEADS, ML_HEADS,
               C_W, C_W,
               D_Q, D_KV, D_KV,
               N_BRANCH * D_MODEL)
SPLIT_IDX = tuple(int(i) for i in np.cumsum(SPLIT_SIZES)[:-1])
N_IN = int(sum(SPLIT_SIZES))

D_FF = 14336
N_EXPERTS = 8
TOP_K = 2
D_EXPERT = 3584
N_DENSE = (DEPTH + 1) // 2
N_MOE = DEPTH // 2

kernel_name = "hybrid_gated_sb_mlstm_sgu_swa_moe"


def rms_norm(x, g):
    xf = x.astype(jnp.float32)
    y = xf * lax.rsqrt(jnp.mean(xf * xf, axis=-1, keepdims=True) + EPS)
    return (y * g.astype(jnp.float32)).astype(x.dtype)


def stick_breaking_attention(q, k, v):
    Bn, S, H, d = q.shape
    nblk = S // BLOCK
    scale = d ** -0.5
    qb = q.reshape(Bn, nblk, BLOCK, H, d).transpose(1, 0, 3, 2, 4)
    kpos = jnp.arange(S)

    def one_block(args):
        qi, blk = args
        z = jnp.einsum('bhqd,bshd->bhqs', qi, k).astype(jnp.float32) * scale
        qpos = blk * BLOCK + jnp.arange(BLOCK)
        causal = kpos[None, :] < qpos[:, None]
        log_beta = jax.nn.log_sigmoid(z)
        log_1mb = jnp.where(causal, jax.nn.log_sigmoid(-z), 0.0)
        shifted = jnp.pad(log_1mb[..., 1:], ((0, 0), (0, 0), (0, 0), (0, 1)))
        tail = lax.cumsum(shifted, axis=3, reverse=True)
        w = jnp.where(causal, jnp.exp(log_beta + tail), 0.0)
        return jnp.einsum('bhqs,bshd->bqhd', w.astype(v.dtype), v)

    out = lax.map(one_block, (qb, jnp.arange(nblk)))
    return out.transpose(1, 0, 2, 3, 4).reshape(Bn, S, H * d)


def causal_depthwise_conv(x, w, b):
    K, C = w.shape
    y = lax.conv_general_dilated(x, w[:, None, :].astype(x.dtype), window_strides=(1,),
                                 padding=[(K - 1, 0)], dimension_numbers=('NWC', 'WIO', 'NWC'),
                                 feature_group_count=C)
    return y + b.astype(x.dtype)


def mlstm_chunkwise(q, k, v, i_pre, log_f):
    Bn, S, H, dk = q.shape
    dv = v.shape[-1]
    L = ML_CHUNK
    n = S // L
    f32 = jnp.float32

    def chunks(t):
        t = t.astype(f32).reshape((Bn, n, L, H) + t.shape[3:])
        return jnp.moveaxis(jnp.moveaxis(t, 1, 0), 3, 2)

    qc, kc, vc = chunks(q), chunks(k * (dk ** -0.5)), chunks(v)
    ic, fc = chunks(i_pre), chunks(log_f)
    tril = jnp.tril(jnp.ones((L, L), dtype=bool))

    def body(carry, inp):
        C, nv, m = carry
        qx, kx, vx, ix, fx = inp
        b = jnp.cumsum(fx, axis=-1)
        log_d = jnp.where(tril, b[..., :, None] - b[..., None, :] + ix[..., None, :], -jnp.inf)
        m_inter = b + m[..., None]
        m_t = jnp.maximum(m_inter, jnp.max(log_d, axis=-1))
        dmat = jnp.exp(log_d - m_t[..., None])
        s = jnp.einsum('bhqd,bhkd->bhqk', qx, kx) * dmat
        inter = jnp.exp(m_inter - m_t)
        num = jnp.einsum('bhqk,bhkv->bhqv', s, vx) + inter[..., None] * jnp.einsum('bhqd,bhdv->bhqv', qx, C)
        den = jnp.sum(s, axis=-1) + inter * jnp.einsum('bhqd,bhd->bhq', qx, nv)
        h = num / jnp.maximum(jnp.abs(den), jnp.exp(-m_t))[..., None]
        m_new = m_t[..., -1]
        w = jnp.exp(b[..., -1:] - b + ix - m_new[..., None])
        decay = jnp.exp(b[..., -1] + m - m_new)
        C_new = decay[..., None, None] * C + jnp.einsum('bhl,bhld,bhlv->bhdv', w, kx, vx)
        n_new = decay[..., None] * nv + jnp.einsum('bhl,bhld->bhd', w, kx)
        return (C_new, n_new, m_new), h

    init = (jnp.zeros((Bn, H, dk, dv), f32), jnp.zeros((Bn, H, dk), f32), jnp.zeros((Bn, H), f32))
    _, hs = lax.scan(body, init, (qc, kc, vc, ic, fc))
    return hs.transpose(1, 0, 3, 2, 4).reshape(Bn, S, H, dv)


def chunked_spatial_gating(u, v, w_s, b_s):
    Bn, S, _ = u.shape
    n = S // SG_CHUNK
    vf = v.astype(jnp.float32)
    mu = jnp.mean(vf, axis=-1, keepdims=True)
    var = jnp.mean(jnp.square(vf - mu), axis=-1, keepdims=True)
    vn = ((vf - mu) * lax.rsqrt(var + EPS)).astype(v.dtype)
    vn = vn.reshape(Bn, n, SG_CHUNK, SG_GROUPS, SG_DIM)
    w = jnp.where(jnp.tril(jnp.ones((SG_CHUNK, SG_CHUNK), dtype=bool)), w_s, 0.0).astype(v.dtype)
    mixed = jnp.einsum('gts,bnsgc->bntgc', w, vn) + b_s.T[:, :, None].astype(v.dtype)
    return (u.reshape(Bn, n, SG_CHUNK, SG_GROUPS, SG_DIM) * mixed).reshape(Bn, S, SG_GROUPS * SG_DIM)


def sliding_window_attention(q, k, v, sinks):
    Bn, S, Hq, d = q.shape
    Hkv = k.shape[2]
    G = Hq // Hkv
    W = SW_WINDOW
    n = S // W
    qb = q.reshape(Bn, n, W, Hkv, G, d)
    kb = k.reshape(Bn, n, W, Hkv, d)
    vb = v.reshape(Bn, n, W, Hkv, d)
    pad = ((0, 0), (1, 0), (0, 0), (0, 0), (0, 0))
    kk = jnp.concatenate([jnp.pad(kb, pad)[:, :-1], kb], axis=2)
    vv = jnp.concatenate([jnp.pad(vb, pad)[:, :-1], vb], axis=2)
    s = jnp.einsum('bnqhgd,bnkhd->bnhgqk', qb, kk).astype(jnp.float32) * (d ** -0.5)
    qi = jnp.arange(W)[:, None]
    kj = jnp.arange(2 * W)[None, :]
    diff = W + qi - kj
    band = (diff >= 0) & (diff < W)
    valid = band[None] & ((jnp.arange(n)[:, None, None] > 0) | (kj >= W)[None])
    s = jnp.where(valid[None, :, None, None], s, -jnp.inf)
    sink = sinks.astype(jnp.float32).reshape(Hkv, G)[None, None, :, :, None, None]
    mx = jnp.maximum(jnp.max(s, axis=-1, keepdims=True), sink)
    p = jnp.exp(s - mx)
    p = p / (jnp.sum(p, axis=-1, keepdims=True) + jnp.exp(sink - mx))
    o = jnp.einsum('bnhgqk,bnkhd->bnqhgd', p.astype(v.dtype), vv)
    return o.reshape(Bn, S, Hq * d)


def hybrid_mixer(h, w_in, conv_w, conv_b, gate_b, head_norm, sg_w, sg_b, q_norm, k_norm, sinks,
                 w_branch, w_out):
    Bn, S, _ = h.shape
    proj = jnp.einsum('bsd,dn->bsn', h, w_in)
    (sb_q, sb_k, sb_v, ml_qk, ml_v, ml_o, ml_i, ml_f,
     sg_u, sg_v, sw_q, sw_k, sw_v, gate_pre) = jnp.split(proj, SPLIT_IDX, axis=-1)

    def heads(t, nh):
        return t.reshape(Bn, S, nh, -1)

    y_a = stick_breaking_attention(heads(sb_q, SB_HEADS), heads(sb_k, SB_HEADS), heads(sb_v, SB_HEADS))
    qk = jax.nn.silu(causal_depthwise_conv(ml_qk, conv_w, conv_b))
    ml_q, ml_k = jnp.split(qk, 2, axis=-1)
    i_pre = ml_i.astype(jnp.float32) + gate_b[0].astype(jnp.float32)
    log_f = jax.nn.log_sigmoid(ml_f.astype(jnp.float32) + gate_b[1].astype(jnp.float32))
    h_b = mlstm_chunkwise(heads(ml_q, ML_HEADS), heads(ml_k, ML_HEADS), heads(ml_v, ML_HEADS), i_pre, log_f)
    h_b = rms_norm(h_b, head_norm).astype(h.dtype).reshape(Bn, S, B_V)
    y_b = jax.nn.sigmoid(ml_o) * h_b
    y_c = chunked_spatial_gating(jax.nn.gelu(sg_u, approximate=False), jax.nn.gelu(sg_v, approximate=False),
                                 sg_w, sg_b)
    y_d = sliding_window_attention(rms_norm(heads(sw_q, SW_QHEADS), q_norm),
                                   rms_norm(heads(sw_k, SW_KVHEADS), k_norm),
                                   heads(sw_v, SW_KVHEADS), sinks)
    gates = jax.nn.sigmoid(gate_pre).reshape(Bn, S, N_BRANCH, D_MODEL)
    merged = jnp.zeros((Bn, S, D_MODEL), h.dtype)
    for i, y in enumerate((y_a, y_b, y_c, y_d)):
        merged = merged + gates[:, :, i] * jnp.einsum('bsm,md->bsd', y, w_branch[i])
    return jnp.einsum('bsd,de->bse', merged, w_out)


def swiglu(h, w_gu, w_down):
    gu = jnp.einsum('bsd,df->bsf', h, w_gu)
    g, u = jnp.split(gu, 2, axis=-1)
    return jnp.einsum('bsf,fd->bsd', jax.nn.silu(g) * u, w_down)


def moe_swiglu(h, router, w_gu, w_down):
    logits = jnp.einsum('bsd,de->bse', h, router).astype(jnp.float32)
    top_val, top_idx = lax.top_k(logits, TOP_K)
    top_w = jax.nn.softmax(top_val, axis=-1)
    gate = jnp.sum(jax.nn.one_hot(top_idx, N_EXPERTS, dtype=jnp.float32) * top_w[..., None], axis=-2)
    gate = gate.astype(h.dtype)
    out = jnp.zeros_like(h)
    for e in range(N_EXPERTS):
        out = out + gate[..., e:e + 1] * swiglu(h, w_gu[e], w_down[e])
    return out


def setup_inputs(seed: int = 0) -> dict:
    key = jax.random.key(seed)
    ks = jax.random.split(key, 24)
    f32 = jnp.float32

    def nrm(k, shape, scale):
        return jax.random.normal(k, shape, f32) * scale

    gate_b = nrm(ks[4], (DEPTH, 2, ML_HEADS), 0.1) + jnp.array([0.0, 3.0], f32)[None, :, None]
    return {
        "x": nrm(ks[0], (BATCH, SEQ, D_MODEL), 1.0),
        "norm_mix": 1.0 + nrm(ks[1], (DEPTH, D_MODEL), 0.02),
        "w_in": nrm(ks[2], (DEPTH, D_MODEL, N_IN), D_MODEL ** -0.5),
        "ml_conv_w": nrm(ks[3], (DEPTH, ML_CONV, 2 * B_QK), ML_CONV ** -0.5),
        "ml_conv_b": nrm(ks[5], (DEPTH, 2 * B_QK), 0.02),
        "ml_gate_b": gate_b,
        "ml_head_norm": 1.0 + nrm(ks[6], (DEPTH, ML_HEADS, ML_DV), 0.02),
        "sg_w": nrm(ks[7], (DEPTH, SG_GROUPS, SG_CHUNK, SG_CHUNK), SG_CHUNK ** -0.5),
        "sg_b": 1.0 + nrm(ks[8], (DEPTH, SG_GROUPS, SG_CHUNK), 0.02),
        "sw_q_norm": 1.0 + nrm(ks[9], (DEPTH, SW_DIM), 0.02),
        "sw_k_norm": 1.0 + nrm(ks[10], (DEPTH, SW_DIM), 0.02),
        "sw_sinks": nrm(ks[11], (DEPTH, SW_QHEADS), 1.0),
        "w_branch": nrm(ks[12], (DEPTH, N_BRANCH, BRANCH_W, D_MODEL), BRANCH_W ** -0.5),
        "w_out": nrm(ks[13], (DEPTH, D_MODEL, D_MODEL), D_MODEL ** -0.5),
        "norm_ffn": 1.0 + nrm(ks[14], (DEPTH, D_MODEL), 0.02),
        "ffn_w_gu": nrm(ks[15], (N_DENSE, D_MODEL, 2 * D_FF), D_MODEL ** -0.5),
        "ffn_w_down": nrm(ks[16], (N_DENSE, D_FF, D_MODEL), D_FF ** -0.5),
        "moe_router": nrm(ks[17], (N_MOE, D_MODEL, N_EXPERTS), D_MODEL ** -0.5),
        "moe_w_gu": nrm(ks[18], (N_MOE, N_EXPERTS, D_MODEL, 2 * D_EXPERT), D_MODEL ** -0.5),
        "moe_w_down": nrm(ks[19], (N_MOE, N_EXPERTS, D_EXPERT, D_MODEL), D_EXPERT ** -0.5),
    }


def reference(x, norm_mix, w_in, ml_conv_w, ml_conv_b, ml_gate_b, ml_head_norm, sg_w, sg_b,
              sw_q_norm, sw_k_norm, sw_sinks, w_branch, w_out, norm_ffn, ffn_w_gu, ffn_w_down,
              moe_router, moe_w_gu, moe_w_down):
    for layer in range(DEPTH):
        h = rms_norm(x, norm_mix[layer])
        x = x + hybrid_mixer(h, w_in[layer], ml_conv_w[layer], ml_conv_b[layer], ml_gate_b[layer],
                             ml_head_norm[layer], sg_w[layer], sg_b[layer], sw_q_norm[layer],
                             sw_k_norm[layer], sw_sinks[layer], w_branch[layer], w_out[layer])
        h = rms_norm(x, norm_ffn[layer])
        j = layer // 2
        if layer % 2 == 0:
            x = x + swiglu(h, ffn_w_gu[j], ffn_w_down[j])
        else:
            x = x + moe_swiglu(h, moe_router[j], moe_w_gu[j], moe_w_down[j])
    return x
```

```python
import functools

import jax
import jax.numpy as jnp
from jax import lax
from jax.experimental import pallas as pl
from jax.experimental.pallas import tpu as pltpu

F32 = jnp.float32
BF16 = jnp.bfloat16
EPS = 1e-6
LANES = 128
CHUNK = 128
VMEM_LIMIT_MB = 56

D_MODEL = 4096
N_HEADS = 8
HEAD_DIM = 128
BRANCH_W = N_HEADS * HEAD_DIM
SW_QHEADS, SW_KVHEADS, SW_DIM = 16, 2, 64
N_BRANCH = 4
N_EXPERTS = 8

CB_SB_Q, CB_SB_K, CB_SB_V = 0, 1, 2
CB_ML_Q, CB_ML_K, CB_ML_V, CB_ML_O = 3, 4, 5, 6
CB_SG_U, CB_SG_V = 7, 8
CB_SW_Q = 9
CB_GATE0 = 10
N_MAIN = (CB_GATE0 + N_BRANCH * D_MODEL // BRANCH_W) * BRANCH_W
N_TAIL = 3 * LANES


def _cparams(sem, vmem_mb=VMEM_LIMIT_MB):
    return pltpu.CompilerParams(dimension_semantics=sem, vmem_limit_bytes=vmem_mb * 1024 * 1024)


def _dot(a, b):
    return jnp.dot(a, b, preferred_element_type=F32)


def _dot_nt(a, b):
    return lax.dot_general(a, b, (((1,), (1,)), ((), ())), preferred_element_type=F32)


def _dot_tn(a, b):
    return lax.dot_general(a, b, (((0,), (0,)), ((), ())), preferred_element_type=F32)


def _dot_f32(a, b):
    return jnp.dot(a, b, preferred_element_type=F32, precision=lax.Precision.HIGHEST)


def _lane_col(x, idx):
    lane = lax.broadcasted_iota(jnp.int32, x.shape, 1)
    return jnp.sum(jnp.where(lane == idx, x, 0.0), axis=1, keepdims=True)


def _softplus(z):
    return jnp.maximum(z, 0.0) + jnp.log1p(jnp.exp(-jnp.abs(z)))


def _rmsnorm_body(x_ref, g_ref, o_ref):
    x = x_ref[...]
    ms = jnp.mean(x * x, axis=-1, keepdims=True)
    o_ref[...] = (x * lax.rsqrt(ms + EPS) * g_ref[...]).astype(o_ref.dtype)


def rmsnorm(x, g, tm=256):
    m, d = x.shape
    return pl.pallas_call(
        _rmsnorm_body,
        grid=(m // tm,),
        in_specs=[pl.BlockSpec((tm, d), lambda i: (i, 0)), pl.BlockSpec((1, d), lambda i: (0, 0))],
        out_specs=pl.BlockSpec((tm, d), lambda i: (i, 0)),
        out_shape=jax.ShapeDtypeStruct((m, d), BF16),
        compiler_params=_cparams(("parallel",)),
    )(x, g.reshape(1, d))


def _inproj_body(a_ref, b_ref, o_ref, *, gate_tile0):
    acc = _dot(a_ref[...], b_ref[...])
    j = pl.program_id(0)

    @pl.when(j >= gate_tile0)
    def _():
        o_ref[...] = jax.nn.sigmoid(acc).astype(o_ref.dtype)

    @pl.when(j < gate_tile0)
    def _():
        o_ref[...] = acc.astype(o_ref.dtype)


def inproj_main(h, w, tm=1024, tn=1024):
    m, k = h.shape
    n = w.shape[1]
    return pl.pallas_call(
        functools.partial(_inproj_body, gate_tile0=CB_GATE0 * BRANCH_W // tn),
        grid=(n // tn, m // tm),
        in_specs=[pl.BlockSpec((tm, k), lambda j, i: (i, 0)), pl.BlockSpec((k, tn), lambda j, i: (0, j))],
        out_specs=pl.BlockSpec((tm, tn), lambda j, i: (i, j)),
        out_shape=jax.ShapeDtypeStruct((m, n), BF16),
        compiler_params=_cparams(("parallel", "parallel")),
    )(h, w)


def _mm_body(a_ref, b_ref, o_ref):
    o_ref[...] = _dot(a_ref[...], b_ref[...]).astype(o_ref.dtype)


def inproj_tail(h, w, tm=1024):
    m, k = h.shape
    n = w.shape[1]
    return pl.pallas_call(
        _mm_body,
        grid=(m // tm,),
        in_specs=[pl.BlockSpec((tm, k), lambda i: (i, 0)), pl.BlockSpec((k, n), lambda i: (0, 0))],
        out_specs=pl.BlockSpec((tm, n), lambda i: (i, 0)),
        out_shape=jax.ShapeDtypeStruct((m, n), F32),
        compiler_params=_cparams(("parallel",)),
    )(h, w)


def _mm_res_body(a_ref, b_ref, x_ref, o_ref):
    o_ref[...] = x_ref[...] + _dot(a_ref[...], b_ref[...])


def matmul_residual(a, w, x, tm=512, tn=1024):
    m, k = a.shape
    n = w.shape[1]
    return pl.pallas_call(
        _mm_res_body,
        grid=(n // tn, m // tm),
        in_specs=[pl.BlockSpec((tm, k), lambda j, i: (i, 0)), pl.BlockSpec((k, tn), lambda j, i: (0, j)),
                  pl.BlockSpec((tm, tn), lambda j, i: (i, j))],
        out_specs=pl.BlockSpec((tm, tn), lambda j, i: (i, j)),
        out_shape=jax.ShapeDtypeStruct((m, n), F32),
        compiler_params=_cparams(("parallel", "parallel")),
    )(a, w, x)


def _merge_body(ya_ref, yb_ref, yc_ref, yd_ref, wa_ref, wb_ref, wc_ref, wd_ref,
                ga_ref, gb_ref, gc_ref, gd_ref, o_ref):
    acc = ga_ref[...].astype(F32) * _dot(ya_ref[...], wa_ref[...])
    acc = acc + gb_ref[...].astype(F32) * _dot(yb_ref[...], wb_ref[...])
    acc = acc + gc_ref[...].astype(F32) * _dot(yc_ref[...], wc_ref[...])
    acc = acc + gd_ref[...].astype(F32) * _dot(yd_ref[...], wd_ref[...])
    o_ref[...] = acc.astype(o_ref.dtype)


def gated_merge(ys, w_branch, proj, tm=512, tn=1024):
    m, kb = ys[0].shape
    d = w_branch.shape[2]
    gpb = d // tn
    g0 = CB_GATE0 * BRANCH_W // tn
    y_specs = [pl.BlockSpec((tm, kb), lambda j, i: (i, 0)) for _ in range(N_BRANCH)]
    w_specs = [pl.BlockSpec((None, kb, tn), functools.partial(lambda j, i, br: (br, 0, j), br=br))
               for br in range(N_BRANCH)]
    g_specs = [pl.BlockSpec((tm, tn), functools.partial(lambda j, i, br: (i, g0 + br * gpb + j), br=br))
               for br in range(N_BRANCH)]
    return pl.pallas_call(
        _merge_body,
        grid=(d // tn, m // tm),
        in_specs=y_specs + w_specs + g_specs,
        out_specs=pl.BlockSpec((tm, tn), lambda j, i: (i, j)),
        out_shape=jax.ShapeDtypeStruct((m, d), BF16),
        compiler_params=_cparams(("parallel", "parallel")),
    )(*ys, w_branch, w_branch, w_branch, w_branch, proj, proj, proj, proj)


def _sb_body(q_ref, k_ref, v_ref, o_ref, *, tq, tk, scale):
    qi = pl.program_id(2)
    r = tq // tk
    q = q_ref[...]
    rows = lax.broadcasted_iota(jnp.int32, (tq, tk), 0)
    cols = lax.broadcasted_iota(jnp.int32, (tq, tk), 1)
    ur = lax.broadcasted_iota(jnp.int32, (tk, tk), 0)
    uc = lax.broadcasted_iota(jnp.int32, (tk, tk), 1)
    upper = jnp.where(ur > uc, 1.0, 0.0).astype(BF16)

    def sub_block(start, acc, c, mask):
        k = k_ref[pl.ds(start, tk), :]
        v = v_ref[pl.ds(start, tk), :]
        z = _dot_nt(q, k) * scale
        sp = _softplus(z)
        l1mb = -sp if mask is None else jnp.where(mask, -sp, 0.0)
        hi = l1mb.astype(BF16)
        lo = (l1mb - hi.astype(F32)).astype(BF16)
        tail = _dot(hi, upper) + _dot(lo, upper)
        w = jnp.exp(z - sp + tail + c)
        if mask is not None:
            w = jnp.where(mask, w, 0.0)
        acc = acc + _dot(w.astype(BF16), v)
        c = c + jnp.sum(l1mb, axis=-1, keepdims=True)
        return acc, c

    acc = jnp.zeros((tq, HEAD_DIM), F32)
    c = jnp.zeros((tq, 1), F32)
    base = pl.multiple_of(qi * tq, tq)
    for j in range(r - 1, -1, -1):
        acc, c = sub_block(base + j * tk, acc, c, (cols + j * tk) < rows)

    def body(it, carry):
        acc, c = carry
        sb0 = pl.multiple_of((qi - 1 - it) * tq, tq)
        for j in range(r - 1, -1, -1):
            acc, c = sub_block(sb0 + j * tk, acc, c, None)
        return acc, c

    acc, c = lax.fori_loop(0, qi, body, (acc, c))
    o_ref[...] = acc.astype(o_ref.dtype)


def sb_attention(proj, batch, seq, tq=512, tk=128):
    m = proj.shape[0]
    nq = seq // tq
    return pl.pallas_call(
        functools.partial(_sb_body, tq=tq, tk=tk, scale=HEAD_DIM ** -0.5),
        grid=(batch, N_HEADS, nq),
        in_specs=[pl.BlockSpec((tq, HEAD_DIM), lambda b, h, i: (b * nq + i, CB_SB_Q * N_HEADS + h)),
                  pl.BlockSpec((seq, HEAD_DIM), lambda b, h, i: (b, CB_SB_K * N_HEADS + h)),
                  pl.BlockSpec((seq, HEAD_DIM), lambda b, h, i: (b, CB_SB_V * N_HEADS + h))],
        out_specs=pl.BlockSpec((tq, HEAD_DIM), lambda b, h, i: (b * nq + i, h)),
        out_shape=jax.ShapeDtypeStruct((m, BRANCH_W), BF16),
        compiler_params=_cparams(("parallel", "parallel", "arbitrary")),
    )(proj, proj, proj)


def _mlstm_body(q_ref, k_ref, v_ref, o_ref, g_ref, cw_ref, cb_ref, gb_ref, hn_ref, y_ref,
                xs_ref, c_ref, n_ref, m_ref, *, n_conv):
    L = CHUNK
    chunk = pl.program_id(1)

    @pl.when(chunk == 0)
    def _():
        xs_ref[0:8, :] = jnp.zeros((8, 2 * BRANCH_W), F32)
        c_ref[...] = jnp.zeros_like(c_ref)
        n_ref[...] = jnp.zeros_like(n_ref)
        m_ref[...] = jnp.zeros_like(m_ref)

    xs_ref[8:8 + L, 0:BRANCH_W] = q_ref[...].astype(F32)
    xs_ref[8:8 + L, BRANCH_W:2 * BRANCH_W] = k_ref[...].astype(F32)
    conv = jnp.zeros((L, 2 * BRANCH_W), F32) + cb_ref[...]
    for j in range(n_conv):
        conv = conv + cw_ref[j:j + 1, :] * xs_ref[pl.ds(8 - (n_conv - 1) + j, L), :]
    xs_ref[0:8, :] = xs_ref[L:L + 8, :]
    qk = conv * jax.nn.sigmoid(conv)
    qf = qk[:, 0:BRANCH_W]
    kf = qk[:, BRANCH_W:2 * BRANCH_W] * (HEAD_DIM ** -0.5)

    g = g_ref[...] + gb_ref[...]
    lf = -_softplus(-g)
    ri = lax.broadcasted_iota(jnp.int32, (L, L), 0)
    ci = lax.broadcasted_iota(jnp.int32, (L, L), 1)
    tril = ri >= ci
    tril_f = jnp.where(tril, 1.0, 0.0).astype(F32)
    triu_f = jnp.where(ri <= ci, 1.0, 0.0).astype(F32)
    b_cols = _dot_f32(tril_f, lf)
    g_t = g.T
    b_rows = _dot_f32(lf.T, triu_f)

    for h in range(N_HEADS):
        sl = slice(h * HEAD_DIM, (h + 1) * HEAD_DIM)
        qh = qf[:, sl]
        kh = kf[:, sl]
        qb = qh.astype(BF16)
        kb = kh.astype(BF16)
        vb = v_ref[:, sl]
        i_col = _lane_col(g, h)
        b_col = _lane_col(b_cols, N_HEADS + h)
        i_row = g_t[h:h + 1, :]
        b_row = b_rows[N_HEADS + h:N_HEADS + h + 1, :]
        m_prev = m_ref[h:h + 1, 0:1]
        c_prev = c_ref[h]
        n_prev = n_ref[h:h + 1, :]

        log_d = jnp.where(tril, b_col - b_row + i_row, -jnp.inf)
        m_inter = b_col + m_prev
        m_t = jnp.maximum(m_inter, jnp.max(log_d, axis=1, keepdims=True))
        dmat = jnp.exp(log_d - m_t)
        s = _dot_nt(qb, kb) * dmat
        inter = jnp.exp(m_inter - m_t)
        num = _dot(s.astype(BF16), vb) + inter * _dot(qb, c_prev.astype(BF16))
        den = jnp.sum(s, axis=1, keepdims=True) + inter * jnp.sum(qb.astype(F32) * n_prev, axis=1, keepdims=True)
        hh = num / jnp.maximum(jnp.abs(den), jnp.exp(-m_t))

        m_new = m_t[L - 1:L, :]
        b_last = b_col[L - 1:L, :]
        w_col = jnp.exp(b_last - b_col + i_col - m_new)
        decay = jnp.exp(b_last + m_prev - m_new)
        kw = kh * w_col
        c_ref[h] = decay * c_prev + _dot_tn(kw.astype(BF16), vb)
        n_ref[h:h + 1, :] = decay * n_prev + jnp.sum(kw, axis=0, keepdims=True)
        m_ref[h:h + 1, :] = jnp.broadcast_to(m_new, (1, LANES))

        hn = hh * lax.rsqrt(jnp.mean(hh * hh, axis=1, keepdims=True) + EPS) * hn_ref[:, sl]
        y_ref[:, sl] = (jax.nn.sigmoid(o_ref[:, sl].astype(F32)) * hn).astype(y_ref.dtype)


def mlstm(proj, tail, conv_w, conv_b, gate_b, head_norm, batch, seq):
    m = proj.shape[0]
    nc = seq // CHUNK
    n_conv = conv_w.shape[0]
    gb = jnp.zeros((1, LANES), F32).at[0, 0:2 * N_HEADS].set(gate_b.reshape(-1))

    def blk(cb):
        return pl.BlockSpec((CHUNK, BRANCH_W), lambda b, c: (b * nc + c, cb))

    def const(shape):
        return pl.BlockSpec(shape, lambda b, c: (0, 0))

    return pl.pallas_call(
        functools.partial(_mlstm_body, n_conv=n_conv),
        grid=(batch, nc),
        in_specs=[blk(CB_ML_Q), blk(CB_ML_K), blk(CB_ML_V), blk(CB_ML_O),
                  pl.BlockSpec((CHUNK, LANES), lambda b, c: (b * nc + c, 2)),
                  const((n_conv, 2 * BRANCH_W)), const((1, 2 * BRANCH_W)), const((1, LANES)),
                  const((1, BRANCH_W))],
        out_specs=pl.BlockSpec((CHUNK, BRANCH_W), lambda b, c: (b * nc + c, 0)),
        out_shape=jax.ShapeDtypeStruct((m, BRANCH_W), BF16),
        scratch_shapes=[pltpu.VMEM((CHUNK + 8, 2 * BRANCH_W), F32),
                        pltpu.VMEM((N_HEADS, HEAD_DIM, HEAD_DIM), F32),
                        pltpu.VMEM((N_HEADS, HEAD_DIM), F32),
                        pltpu.VMEM((N_HEADS, LANES), F32)],
        compiler_params=_cparams(("parallel", "arbitrary")),
    )(proj, proj, proj, proj, tail, conv_w, conv_b.reshape(1, -1), gb, head_norm.reshape(1, -1))


def _gelu(x):
    return 0.5 * x * (1.0 + lax.erf(x * (2.0 ** -0.5)))


def _sgu_body(u_ref, v_ref, w_ref, b_ref, o_ref):
    L = CHUNK
    u = _gelu(u_ref[...].astype(F32))
    v = _gelu(v_ref[...].astype(F32))
    mu = jnp.mean(v, axis=1, keepdims=True)
    vc = v - mu
    var = jnp.mean(vc * vc, axis=1, keepdims=True)
    vn = (vc * lax.rsqrt(var + EPS)).astype(BF16)
    ri = lax.broadcasted_iota(jnp.int32, (L, L), 0)
    ci = lax.broadcasted_iota(jnp.int32, (L, L), 1)
    tril = ri >= ci
    bias = b_ref[...]
    for g in range(N_HEADS):
        sl = slice(g * HEAD_DIM, (g + 1) * HEAD_DIM)
        wg = jnp.where(tril, w_ref[g], 0.0).astype(BF16)
        mixed = _dot(wg, vn[:, sl]) + _lane_col(bias, g)
        o_ref[:, sl] = (u[:, sl] * mixed).astype(o_ref.dtype)


def spatial_gating(proj, sg_w, sg_b):
    m = proj.shape[0]
    bias = jnp.zeros((CHUNK, LANES), F32).at[:, 0:N_HEADS].set(sg_b.T)
    return pl.pallas_call(
        _sgu_body,
        grid=(m // CHUNK,),
        in_specs=[pl.BlockSpec((CHUNK, BRANCH_W), lambda i: (i, CB_SG_U)),
                  pl.BlockSpec((CHUNK, BRANCH_W), lambda i: (i, CB_SG_V)),
                  pl.BlockSpec((N_HEADS, CHUNK, CHUNK), lambda i: (0, 0, 0)),
                  pl.BlockSpec((CHUNK, LANES), lambda i: (0, 0))],
        out_specs=pl.BlockSpec((CHUNK, BRANCH_W), lambda i: (i, 0)),
        out_shape=jax.ShapeDtypeStruct((m, BRANCH_W), BF16),
        compiler_params=_cparams(("parallel",)),
    )(proj, proj, sg_w, bias)


def _swa_body(q_ref, kp_ref, kc_ref, vp_ref, vc_ref, qg_ref, kg_ref, sink_ref, o_ref, *, n_chunks):
    W = CHUNK
    n = pl.program_id(0) % n_chunks
    lane = lax.broadcasted_iota(jnp.int32, (1, LANES), 1)
    lo = lane < SW_DIM

    def pair_norm(x, gain):
        x2 = x * x
        s_lo = jnp.sum(jnp.where(lo, x2, 0.0), axis=1, keepdims=True)
        s_hi = jnp.sum(jnp.where(lo, 0.0, x2), axis=1, keepdims=True)
        ms = jnp.where(lo, s_lo, s_hi) * (1.0 / SW_DIM)
        return x * lax.rsqrt(ms + EPS) * gain

    kg = kg_ref[...]
    kk = jnp.concatenate([pair_norm(kp_ref[...], kg), pair_norm(kc_ref[...], kg)], axis=0)
    vv = jnp.concatenate([vp_ref[...], vc_ref[...]], axis=0)
    kk_sw = pltpu.roll(kk, SW_DIM, axis=1)
    vv_sw = pltpu.roll(vv, SW_DIM, axis=1)
    k_ver = [[jnp.where(lo, kk, 0.0).astype(BF16), jnp.where(lo, 0.0, kk_sw).astype(BF16)],
             [jnp.where(lo, kk_sw, 0.0).astype(BF16), jnp.where(lo, 0.0, kk).astype(BF16)]]
    v_ver = [[jnp.where(lo, vv, 0.0).astype(BF16), jnp.where(lo, 0.0, vv_sw).astype(BF16)],
             [jnp.where(lo, vv_sw, 0.0).astype(BF16), jnp.where(lo, 0.0, vv).astype(BF16)]]

    qi = lax.broadcasted_iota(jnp.int32, (W, 2 * W), 0)
    kj = lax.broadcasted_iota(jnp.int32, (W, 2 * W), 1)
    diff = W + qi - kj
    valid = (diff >= 0) & (diff < W) & ((n > 0) | (kj >= W))
    scale = SW_DIM ** -0.5
    group = SW_QHEADS // SW_KVHEADS
    qg = qg_ref[...]
    for pb in range(SW_QHEADS // 2):
        sl = slice(pb * LANES, (pb + 1) * LANES)
        qn = pair_norm(q_ref[:, sl].astype(F32), qg)
        out = jnp.zeros((W, LANES), F32)
        for half in range(2):
            head = 2 * pb + half
            hk = head // group
            qm = jnp.where(lo if half == 0 else jnp.logical_not(lo), qn, 0.0).astype(BF16)
            s = jnp.where(valid, _dot_nt(qm, k_ver[hk][half]) * scale, -jnp.inf)
            sink = sink_ref[head:head + 1, 0:1]
            mx = jnp.maximum(jnp.max(s, axis=1, keepdims=True), sink)
            p = jnp.exp(s - mx)
            p = p / (jnp.sum(p, axis=1, keepdims=True) + jnp.exp(sink - mx))
            out = out + _dot(p.astype(BF16), v_ver[hk][half])
        o_ref[:, sl] = out.astype(o_ref.dtype)


def swa_attention(proj, tail, q_norm, k_norm, sinks, seq):
    m = proj.shape[0]
    nc = seq // CHUNK
    qg = jnp.tile(q_norm.reshape(1, SW_DIM), (1, LANES // SW_DIM))
    kg = jnp.tile(k_norm.reshape(1, SW_DIM), (1, LANES // SW_DIM))
    sink_b = jnp.broadcast_to(sinks.reshape(SW_QHEADS, 1), (SW_QHEADS, LANES)).astype(F32)

    def prev(i):
        return jnp.maximum(i - 1, 0)

    return pl.pallas_call(
        functools.partial(_swa_body, n_chunks=nc),
        grid=(m // CHUNK,),
        in_specs=[pl.BlockSpec((CHUNK, BRANCH_W), lambda i: (i, CB_SW_Q)),
                  pl.BlockSpec((CHUNK, LANES), lambda i: (prev(i), 0)),
                  pl.BlockSpec((CHUNK, LANES), lambda i: (i, 0)),
                  pl.BlockSpec((CHUNK, LANES), lambda i: (prev(i), 1)),
                  pl.BlockSpec((CHUNK, LANES), lambda i: (i, 1)),
                  pl.BlockSpec((1, LANES), lambda i: (0, 0)),
                  pl.BlockSpec((1, LANES), lambda i: (0, 0)),
                  pl.BlockSpec((SW_QHEADS, LANES), lambda i: (0, 0))],
        out_specs=pl.BlockSpec((CHUNK, BRANCH_W), lambda i: (i, 0)),
        out_shape=jax.ShapeDtypeStruct((m, BRANCH_W), BF16),
        compiler_params=_cparams(("parallel",)),
    )(proj, tail, tail, tail, tail, qg, kg, sink_b)


def _ffn_gu_body(a_ref, bg_ref, bu_ref, o_ref):
    a = a_ref[...]
    g = _dot(a, bg_ref[...])
    u = _dot(a, bu_ref[...])
    o_ref[...] = (g * jax.nn.sigmoid(g) * u).astype(o_ref.dtype)


def ffn_gate_up(h, w_gu, tm=1024, tn=512):
    m, k = h.shape
    f = w_gu.shape[1] // 2
    nj = f // tn
    return pl.pallas_call(
        _ffn_gu_body,
        grid=(nj, m // tm),
        in_specs=[pl.BlockSpec((tm, k), lambda j, i: (i, 0)),
                  pl.BlockSpec((k, tn), lambda j, i: (0, j)),
                  pl.BlockSpec((k, tn), lambda j, i: (0, nj + j))],
        out_specs=pl.BlockSpec((tm, tn), lambda j, i: (i, j)),
        out_shape=jax.ShapeDtypeStruct((m, f), BF16),
        compiler_params=_cparams(("parallel", "parallel")),
    )(h, w_gu, w_gu)


def _ffn_down_body(a_ref, b_ref, x_ref, *rest, nk, scale_col):
    if scale_col is None:
        o_ref, acc_ref = rest
    else:
        s_ref, o_ref, acc_ref = rest
    kk = pl.program_id(2)
    part = _dot(a_ref[...], b_ref[...])

    @pl.when(kk == 0)
    def _():
        acc_ref[...] = part

    @pl.when(kk > 0)
    def _():
        acc_ref[...] += part

    @pl.when(kk == nk - 1)
    def _():
        y = acc_ref[...]
        if scale_col is not None:
            y = y * _lane_col(s_ref[...], scale_col)
        o_ref[...] = x_ref[...] + y


def ffn_down(act, w_down, x, row_scale=None, scale_col=None, tm=1024, tn=1024, tk=2048):
    m, f = act.shape
    n = w_down.shape[1]
    nk = f // tk
    in_specs = [pl.BlockSpec((tm, tk), lambda j, i, k: (i, k)),
                pl.BlockSpec((tk, tn), lambda j, i, k: (k, j)),
                pl.BlockSpec((tm, tn), lambda j, i, k: (i, j))]
    args = [act, w_down, x]
    if row_scale is not None:
        in_specs.append(pl.BlockSpec((tm, LANES), lambda j, i, k: (i, 0)))
        args.append(row_scale)
    return pl.pallas_call(
        functools.partial(_ffn_down_body, nk=nk, scale_col=scale_col if row_scale is not None else None),
        grid=(n // tn, m // tm, nk),
        in_specs=in_specs,
        out_specs=pl.BlockSpec((tm, tn), lambda j, i, k: (i, j)),
        out_shape=jax.ShapeDtypeStruct((m, n), F32),
        scratch_shapes=[pltpu.VMEM((tm, tn), F32)],
        compiler_params=_cparams(("parallel", "parallel", "arbitrary")),
    )(*args)


def _router_body(x_ref, g_ref, r_ref, h_ref, gate_ref):
    x = x_ref[...]
    ms = jnp.mean(x * x, axis=-1, keepdims=True)
    h = x * lax.rsqrt(ms + EPS) * g_ref[...]
    h_ref[...] = h.astype(h_ref.dtype)
    logits = _dot_f32(h, r_ref[...])
    lane = lax.broadcasted_iota(jnp.int32, logits.shape, 1)
    logits = jnp.where(lane < N_EXPERTS, logits, -jnp.inf)
    m1 = jnp.max(logits, axis=1, keepdims=True)
    i1 = jnp.min(jnp.where(logits == m1, lane, LANES), axis=1, keepdims=True)
    rest = jnp.where(lane == i1, -jnp.inf, logits)
    m2 = jnp.max(rest, axis=1, keepdims=True)
    i2 = jnp.min(jnp.where(rest == m2, lane, LANES), axis=1, keepdims=True)
    e2 = jnp.exp(m2 - m1)
    w1 = 1.0 / (1.0 + e2)
    w2 = e2 / (1.0 + e2)
    gate_ref[...] = jnp.where(lane == i1, w1, 0.0) + jnp.where(lane == i2, w2, 0.0)


def norm_and_route(x, g, router, tm=256):
    m, d = x.shape
    r_pad = jnp.zeros((d, LANES), F32).at[:, 0:N_EXPERTS].set(router)
    return pl.pallas_call(
        _router_body,
        grid=(m // tm,),
        in_specs=[pl.BlockSpec((tm, d), lambda i: (i, 0)), pl.BlockSpec((1, d), lambda i: (0, 0)),
                  pl.BlockSpec((d, LANES), lambda i: (0, 0))],
        out_specs=[pl.BlockSpec((tm, d), lambda i: (i, 0)), pl.BlockSpec((tm, LANES), lambda i: (i, 0))],
        out_shape=[jax.ShapeDtypeStruct((m, d), BF16), jax.ShapeDtypeStruct((m, LANES), F32)],
        compiler_params=_cparams(("parallel",)),
    )(x, g.reshape(1, d), r_pad)


def _split_w_in(w_in):
    a_w = BRANCH_W
    o_ml = 3 * a_w
    o_if = o_ml + 4 * a_w
    o_sg = o_if + 2 * N_HEADS
    o_swq = o_sg + 2 * a_w
    o_swk = o_swq + SW_QHEADS * SW_DIM
    o_gate = o_swk + 2 * SW_KVHEADS * SW_DIM
    main = jnp.concatenate([w_in[:, :o_if], w_in[:, o_sg:o_swk], w_in[:, o_gate:]], axis=1).astype(BF16)
    pad = jnp.zeros((w_in.shape[0], LANES - 2 * N_HEADS), w_in.dtype)
    tail = jnp.concatenate([w_in[:, o_swk:o_gate], w_in[:, o_if:o_sg], pad], axis=1).astype(BF16)
    return main, tail


def _mixer(h, layer_params, batch, seq):
    (w_in, conv_w, conv_b, gate_b, head_norm, sg_w, sg_b, q_norm, k_norm, sinks, w_branch, w_out) = layer_params
    w_main, w_tail = _split_w_in(w_in)
    proj = inproj_main(h, w_main)
    tail = inproj_tail(h, w_tail)
    y_a = sb_attention(proj, batch, seq)
    y_b = mlstm(proj, tail, conv_w, conv_b, gate_b, head_norm, batch, seq)
    y_c = spatial_gating(proj, sg_w, sg_b)
    y_d = swa_attention(proj, tail, q_norm, k_norm, sinks, seq)
    merged = gated_merge((y_a, y_b, y_c, y_d), w_branch.astype(BF16), proj)
    return merged, w_out.astype(BF16)


def kernel(x, norm_mix, w_in, ml_conv_w, ml_conv_b, ml_gate_b, ml_head_norm, sg_w, sg_b, sw_q_norm, sw_k_norm,
           sw_sinks, w_branch, w_out, norm_ffn, ffn_w_gu, ffn_w_down, moe_router, moe_w_gu, moe_w_down):
    batch, seq, d = x.shape
    depth = norm_mix.shape[0]
    xf = x.reshape(batch * seq, d)
    for layer in range(depth):
        h = rmsnorm(xf, norm_mix[layer])
        merged, w_o = _mixer(h, (w_in[layer], ml_conv_w[layer], ml_conv_b[layer], ml_gate_b[layer],
                                 ml_head_norm[layer], sg_w[layer], sg_b[layer], sw_q_norm[layer],
                                 sw_k_norm[layer], sw_sinks[layer], w_branch[layer], w_out[layer]),
                             batch, seq)
        xf = matmul_residual(merged, w_o, xf)
        j = layer // 2
        if layer % 2 == 0:
            h = rmsnorm(xf, norm_ffn[layer])
            act = ffn_gate_up(h, ffn_w_gu[j].astype(BF16))
            xf = ffn_down(act, ffn_w_down[j].astype(BF16), xf)
        else:
            h, gate = norm_and_route(xf, norm_ffn[layer], moe_router[j])
            for e in range(N_EXPERTS):
                act = ffn_gate_up(h, moe_w_gu[j, e].astype(BF16))
                xf = ffn_down(act, moe_w_down[j, e].astype(BF16), xf, row_scale=gate, scale_col=e, tk=1792)
    return xf.reshape(batch, seq, d)
```

```python
import functools

import jax
import jax.numpy as jnp
from jax import lax
from jax.experimental import pallas as pl
from jax.experimental.pallas import tpu as pltpu

F32 = jnp.float32
BF16 = jnp.bfloat16
EPS = 1e-6
LANES = 128
CHUNK = 128
VMEM_LIMIT_MB = 56

D_MODEL = 4096
N_HEADS = 8
HEAD_DIM = 128
BRANCH_W = N_HEADS * HEAD_DIM
SW_QHEADS, SW_KVHEADS, SW_DIM = 16, 2, 64
N_BRANCH = 4
N_EXPERTS = 8

CB_SB_Q, CB_SB_K, CB_SB_V = 0, 1, 2
CB_ML_Q, CB_ML_K, CB_ML_V, CB_ML_O = 3, 4, 5, 6
CB_SG_U, CB_SG_V = 7, 8
CB_SW_Q = 9
CB_GATE0 = 10
N_MAIN = (CB_GATE0 + N_BRANCH * D_MODEL // BRANCH_W) * BRANCH_W
N_TAIL = 3 * LANES


def _cparams(sem, vmem_mb=VMEM_LIMIT_MB):
    return pltpu.CompilerParams(dimension_semantics=sem, vmem_limit_bytes=vmem_mb * 1024 * 1024)


def _dot(a, b):
    return jnp.dot(a, b, preferred_element_type=F32)


def _dot_nt(a, b):
    return lax.dot_general(a, b, (((1,), (1,)), ((), ())), preferred_element_type=F32)


def _dot_tn(a, b):
    return lax.dot_general(a, b, (((0,), (0,)), ((), ())), preferred_element_type=F32)


def _dot_f32(a, b):
    return jnp.dot(a, b, preferred_element_type=F32, precision=lax.Precision.HIGHEST)


def _lane_col(x, idx):
    lane = lax.broadcasted_iota(jnp.int32, x.shape, 1)
    return jnp.sum(jnp.where(lane == idx, x, 0.0), axis=1, keepdims=True)


def _softplus(z):
    return jnp.maximum(z, 0.0) + jnp.log(1.0 + jnp.exp(-jnp.abs(z)))


def _rmsnorm_body(x_ref, g_ref, o_ref):
    x = x_ref[...]
    ms = jnp.mean(x * x, axis=-1, keepdims=True)
    o_ref[...] = (x * lax.rsqrt(ms + EPS) * g_ref[...]).astype(o_ref.dtype)


def rmsnorm(x, g, tm=256):
    m, d = x.shape
    return pl.pallas_call(
        _rmsnorm_body,
        grid=(m // tm,),
        in_specs=[pl.BlockSpec((tm, d), lambda i: (i, 0)), pl.BlockSpec((1, d), lambda i: (0, 0))],
        out_specs=pl.BlockSpec((tm, d), lambda i: (i, 0)),
        out_shape=jax.ShapeDtypeStruct((m, d), BF16),
        compiler_params=_cparams(("parallel",)),
    )(x, g.reshape(1, d))


def _inproj_body(a_ref, b_ref, o_ref, *, gate_tile0):
    acc = _dot(a_ref[...], b_ref[...])
    j = pl.program_id(0)

    @pl.when(j >= gate_tile0)
    def _():
        o_ref[...] = jax.nn.sigmoid(acc).astype(o_ref.dtype)

    @pl.when(j < gate_tile0)
    def _():
        o_ref[...] = acc.astype(o_ref.dtype)


def inproj_main(h, w, tm=1024, tn=1024):
    m, k = h.shape
    n = w.shape[1]
    return pl.pallas_call(
        functools.partial(_inproj_body, gate_tile0=CB_GATE0 * BRANCH_W // tn),
        grid=(n // tn, m // tm),
        in_specs=[pl.BlockSpec((tm, k), lambda j, i: (i, 0)), pl.BlockSpec((k, tn), lambda j, i: (0, j))],
        out_specs=pl.BlockSpec((tm, tn), lambda j, i: (i, j)),
        out_shape=jax.ShapeDtypeStruct((m, n), BF16),
        compiler_params=_cparams(("parallel", "parallel")),
    )(h, w)


def _mm_body(a_ref, b_ref, o_ref):
    o_ref[...] = _dot(a_ref[...], b_ref[...]).astype(o_ref.dtype)


def inproj_tail(h, w, tm=1024):
    m, k = h.shape
    n = w.shape[1]
    return pl.pallas_call(
        _mm_body,
        grid=(m // tm,),
        in_specs=[pl.BlockSpec((tm, k), lambda i: (i, 0)), pl.BlockSpec((k, n), lambda i: (0, 0))],
        out_specs=pl.BlockSpec((tm, n), lambda i: (i, 0)),
        out_shape=jax.ShapeDtypeStruct((m, n), F32),
        compiler_params=_cparams(("parallel",)),
    )(h, w)


def _mm_res_body(a_ref, b_ref, x_ref, o_ref):
    o_ref[...] = x_ref[...] + _dot(a_ref[...], b_ref[...])


def matmul_residual(a, w, x, tm=512, tn=1024):
    m, k = a.shape
    n = w.shape[1]
    return pl.pallas_call(
        _mm_res_body,
        grid=(n // tn, m // tm),
        in_specs=[pl.BlockSpec((tm, k), lambda j, i: (i, 0)), pl.BlockSpec((k, tn), lambda j, i: (0, j)),
                  pl.BlockSpec((tm, tn), lambda j, i: (i, j))],
        out_specs=pl.BlockSpec((tm, tn), lambda j, i: (i, j)),
        out_shape=jax.ShapeDtypeStruct((m, n), F32),
        compiler_params=_cparams(("parallel", "parallel")),
    )(a, w, x)


def _merge_body(ya_ref, yb_ref, yc_ref, yd_ref, wa_ref, wb_ref, wc_ref, wd_ref,
                ga_ref, gb_ref, gc_ref, gd_ref, o_ref):
    acc = ga_ref[...].astype(F32) * _dot(ya_ref[...], wa_ref[...])
    acc = acc + gb_ref[...].astype(F32) * _dot(yb_ref[...], wb_ref[...])
    acc = acc + gc_ref[...].astype(F32) * _dot(yc_ref[...], wc_ref[...])
    acc = acc + gd_ref[...].astype(F32) * _dot(yd_ref[...], wd_ref[...])
    o_ref[...] = acc.astype(o_ref.dtype)


def gated_merge(ys, w_branch, proj, tm=512, tn=1024):
    m, kb = ys[0].shape
    d = w_branch.shape[2]
    gpb = d // tn
    g0 = CB_GATE0 * BRANCH_W // tn
    y_specs = [pl.BlockSpec((tm, kb), lambda j, i: (i, 0)) for _ in range(N_BRANCH)]
    w_specs = [pl.BlockSpec((None, kb, tn), functools.partial(lambda j, i, br: (br, 0, j), br=br))
               for br in range(N_BRANCH)]
    g_specs = [pl.BlockSpec((tm, tn), functools.partial(lambda j, i, br: (i, g0 + br * gpb + j), br=br))
               for br in range(N_BRANCH)]
    return pl.pallas_call(
        _merge_body,
        grid=(d // tn, m // tm),
        in_specs=y_specs + w_specs + g_specs,
        out_specs=pl.BlockSpec((tm, tn), lambda j, i: (i, j)),
        out_shape=jax.ShapeDtypeStruct((m, d), BF16),
        compiler_params=_cparams(("parallel", "parallel")),
    )(*ys, w_branch, w_branch, w_branch, w_branch, proj, proj, proj, proj)


def _sb_body(q_ref, k_ref, v_ref, o_ref, *, tq, tk, scale):
    qi = pl.program_id(2)
    r = tq // tk
    q = q_ref[...]
    rows = lax.broadcasted_iota(jnp.int32, (tq, tk), 0)
    cols = lax.broadcasted_iota(jnp.int32, (tq, tk), 1)
    ur = lax.broadcasted_iota(jnp.int32, (tk, tk), 0)
    uc = lax.broadcasted_iota(jnp.int32, (tk, tk), 1)
    upper = jnp.where(ur > uc, 1.0, 0.0).astype(BF16)

    def sub_block(start, acc, c, mask):
        k = k_ref[pl.ds(start, tk), :]
        v = v_ref[pl.ds(start, tk), :]
        z = _dot_nt(q, k) * scale
        sp = _softplus(z)
        l1mb = -sp if mask is None else jnp.where(mask, -sp, 0.0)
        hi = l1mb.astype(BF16)
        lo = (l1mb - hi.astype(F32)).astype(BF16)
        tail = _dot(hi, upper) + _dot(lo, upper)
        w = jnp.exp(z - sp + tail + c)
        if mask is not None:
            w = jnp.where(mask, w, 0.0)
        acc = acc + _dot(w.astype(BF16), v)
        c = c + jnp.sum(l1mb, axis=-1, keepdims=True)
        return acc, c

    acc = jnp.zeros((tq, HEAD_DIM), F32)
    c = jnp.zeros((tq, 1), F32)
    base = pl.multiple_of(qi * tq, tq)
    for j in range(r - 1, -1, -1):
        acc, c = sub_block(base + j * tk, acc, c, (cols + j * tk) < rows)

    def body(it, carry):
        acc, c = carry
        sb0 = pl.multiple_of((qi - 1 - it) * tq, tq)
        for j in range(r - 1, -1, -1):
            acc, c = sub_block(sb0 + j * tk, acc, c, None)
        return acc, c

    acc, c = lax.fori_loop(0, qi, body, (acc, c))
    o_ref[...] = acc.astype(o_ref.dtype)


def sb_attention(proj, batch, seq, tq=512, tk=128):
    m = proj.shape[0]
    nq = seq // tq
    return pl.pallas_call(
        functools.partial(_sb_body, tq=tq, tk=tk, scale=HEAD_DIM ** -0.5),
        grid=(batch, N_HEADS, nq),
        in_specs=[pl.BlockSpec((tq, HEAD_DIM), lambda b, h, i: (b * nq + i, CB_SB_Q * N_HEADS + h)),
                  pl.BlockSpec((seq, HEAD_DIM), lambda b, h, i: (b, CB_SB_K * N_HEADS + h)),
                  pl.BlockSpec((seq, HEAD_DIM), lambda b, h, i: (b, CB_SB_V * N_HEADS + h))],
        out_specs=pl.BlockSpec((tq, HEAD_DIM), lambda b, h, i: (b * nq + i, h)),
        out_shape=jax.ShapeDtypeStruct((m, BRANCH_W), BF16),
        compiler_params=_cparams(("parallel", "parallel", "arbitrary")),
    )(proj, proj, proj)


def _mlstm_body(q_ref, k_ref, v_ref, o_ref, g_ref, cw_ref, cb_ref, gb_ref, hn_ref, y_ref,
                xs_ref, c_ref, n_ref, m_ref, *, n_conv):
    L = CHUNK
    chunk = pl.program_id(1)

    @pl.when(chunk == 0)
    def _():
        xs_ref[0:8, :] = jnp.zeros((8, 2 * BRANCH_W), F32)
        c_ref[...] = jnp.zeros_like(c_ref)
        n_ref[...] = jnp.zeros_like(n_ref)
        m_ref[...] = jnp.zeros_like(m_ref)

    xs_ref[8:8 + L, 0:BRANCH_W] = q_ref[...].astype(F32)
    xs_ref[8:8 + L, BRANCH_W:2 * BRANCH_W] = k_ref[...].astype(F32)
    conv = jnp.zeros((L, 2 * BRANCH_W), F32) + cb_ref[...]
    for j in range(n_conv):
        conv = conv + cw_ref[j:j + 1, :] * xs_ref[pl.ds(8 - (n_conv - 1) + j, L), :]
    xs_ref[0:8, :] = xs_ref[L:L + 8, :]
    qk = conv * jax.nn.sigmoid(conv)
    qf = qk[:, 0:BRANCH_W]
    kf = qk[:, BRANCH_W:2 * BRANCH_W] * (HEAD_DIM ** -0.5)

    g = g_ref[...] + gb_ref[...]
    lf = -_softplus(-g)
    ri = lax.broadcasted_iota(jnp.int32, (L, L), 0)
    ci = lax.broadcasted_iota(jnp.int32, (L, L), 1)
    tril = ri >= ci
    tril_f = jnp.where(tril, 1.0, 0.0).astype(F32)
    triu_f = jnp.where(ri <= ci, 1.0, 0.0).astype(F32)
    b_cols = _dot_f32(tril_f, lf)
    g_t = g.T
    b_rows = _dot_f32(lf.T, triu_f)

    for h in range(N_HEADS):
        sl = slice(h * HEAD_DIM, (h + 1) * HEAD_DIM)
        qh = qf[:, sl]
        kh = kf[:, sl]
        qb = qh.astype(BF16)
        kb = kh.astype(BF16)
        vb = v_ref[:, sl]
        i_col = _lane_col(g, h)
        b_col = _lane_col(b_cols, N_HEADS + h)
        i_row = g_t[h:h + 1, :]
        b_row = b_rows[N_HEADS + h:N_HEADS + h + 1, :]
        m_prev = m_ref[h:h + 1, 0:1]
        c_prev = c_ref[h]
        n_prev = n_ref[h:h + 1, :]

        log_d = jnp.where(tril, b_col - b_row + i_row, -jnp.inf)
        m_inter = b_col + m_prev
        m_t = jnp.maximum(m_inter, jnp.max(log_d, axis=1, keepdims=True))
        dmat = jnp.exp(log_d - m_t)
        s = _dot_nt(qb, kb) * dmat
        inter = jnp.exp(m_inter - m_t)
        num = _dot(s.astype(BF16), vb) + inter * _dot(qb, c_prev.astype(BF16))
        den = jnp.sum(s, axis=1, keepdims=True) + inter * jnp.sum(qb.astype(F32) * n_prev, axis=1, keepdims=True)
        hh = num / jnp.maximum(jnp.abs(den), jnp.exp(-m_t))

        m_new = m_t[L - 1:L, :]
        b_last = b_col[L - 1:L, :]
        w_col = jnp.exp(b_last - b_col + i_col - m_new)
        decay = jnp.exp(b_last + m_prev - m_new)
        kw = kh * w_col
        c_ref[h] = decay * c_prev + _dot_tn(kw.astype(BF16), vb)
        n_ref[h:h + 1, :] = decay * n_prev + jnp.sum(kw, axis=0, keepdims=True)
        m_ref[h:h + 1, :] = jnp.broadcast_to(m_new, (1, LANES))

        hn = hh * lax.rsqrt(jnp.mean(hh * hh, axis=1, keepdims=True) + EPS) * hn_ref[:, sl]
        y_ref[:, sl] = (jax.nn.sigmoid(o_ref[:, sl].astype(F32)) * hn).astype(y_ref.dtype)


def mlstm(proj, tail, conv_w, conv_b, gate_b, head_norm, batch, seq):
    m = proj.shape[0]
    nc = seq // CHUNK
    n_conv = conv_w.shape[0]
    gb = jnp.zeros((1, LANES), F32).at[0, 0:2 * N_HEADS].set(gate_b.reshape(-1))

    def blk(cb):
        return pl.BlockSpec((CHUNK, BRANCH_W), lambda b, c: (b * nc + c, cb))

    def const(shape):
        return pl.BlockSpec(shape, lambda b, c: (0, 0))

    return pl.pallas_call(
        functools.partial(_mlstm_body, n_conv=n_conv),
        grid=(batch, nc),
        in_specs=[blk(CB_ML_Q), blk(CB_ML_K), blk(CB_ML_V), blk(CB_ML_O),
                  pl.BlockSpec((CHUNK, LANES), lambda b, c: (b * nc + c, 2)),
                  const((n_conv, 2 * BRANCH_W)), const((1, 2 * BRANCH_W)), const((1, LANES)),
                  const((1, BRANCH_W))],
        out_specs=pl.BlockSpec((CHUNK, BRANCH_W), lambda b, c: (b * nc + c, 0)),
        out_shape=jax.ShapeDtypeStruct((m, BRANCH_W), BF16),
        scratch_shapes=[pltpu.VMEM((CHUNK + 8, 2 * BRANCH_W), F32),
                        pltpu.VMEM((N_HEADS, HEAD_DIM, HEAD_DIM), F32),
                        pltpu.VMEM((N_HEADS, HEAD_DIM), F32),
                        pltpu.VMEM((N_HEADS, LANES), F32)],
        compiler_params=_cparams(("parallel", "arbitrary")),
    )(proj, proj, proj, proj, tail, conv_w, conv_b.reshape(1, -1), gb, head_norm.reshape(1, -1))


def _gelu(x):
    return 0.5 * x * (1.0 + lax.erf(x * (2.0 ** -0.5)))


def _sgu_body(u_ref, v_ref, w_ref, b_ref, o_ref):
    L = CHUNK
    u = _gelu(u_ref[...].astype(F32))
    v = _gelu(v_ref[...].astype(F32))
    mu = jnp.mean(v, axis=1, keepdims=True)
    vc = v - mu
    var = jnp.mean(vc * vc, axis=1, keepdims=True)
    vn = (vc * lax.rsqrt(var + EPS)).astype(BF16)
    ri = lax.broadcasted_iota(jnp.int32, (L, L), 0)
    ci = lax.broadcasted_iota(jnp.int32, (L, L), 1)
    tril = ri >= ci
    bias = b_ref[...]
    for g in range(N_HEADS):
        sl = slice(g * HEAD_DIM, (g + 1) * HEAD_DIM)
        wg = jnp.where(tril, w_ref[g], 0.0).astype(BF16)
        mixed = _dot(wg, vn[:, sl]) + _lane_col(bias, g)
        o_ref[:, sl] = (u[:, sl] * mixed).astype(o_ref.dtype)


def spatial_gating(proj, sg_w, sg_b):
    m = proj.shape[0]
    bias = jnp.zeros((CHUNK, LANES), F32).at[:, 0:N_HEADS].set(sg_b.T)
    return pl.pallas_call(
        _sgu_body,
        grid=(m // CHUNK,),
        in_specs=[pl.BlockSpec((CHUNK, BRANCH_W), lambda i: (i, CB_SG_U)),
                  pl.BlockSpec((CHUNK, BRANCH_W), lambda i: (i, CB_SG_V)),
                  pl.BlockSpec((N_HEADS, CHUNK, CHUNK), lambda i: (0, 0, 0)),
                  pl.BlockSpec((CHUNK, LANES), lambda i: (0, 0))],
        out_specs=pl.BlockSpec((CHUNK, BRANCH_W), lambda i: (i, 0)),
        out_shape=jax.ShapeDtypeStruct((m, BRANCH_W), BF16),
        compiler_params=_cparams(("parallel",)),
    )(proj, proj, sg_w, bias)


def _swa_body(q_ref, kp_ref, kc_ref, vp_ref, vc_ref, qg_ref, kg_ref, sink_ref, o_ref, *, n_chunks):
    W = CHUNK
    n = pl.program_id(0) % n_chunks
    lane = lax.broadcasted_iota(jnp.int32, (1, LANES), 1)
    lo = lane < SW_DIM

    def pair_norm(x, gain):
        x2 = x * x
        s_lo = jnp.sum(jnp.where(lo, x2, 0.0), axis=1, keepdims=True)
        s_hi = jnp.sum(jnp.where(lo, 0.0, x2), axis=1, keepdims=True)
        ms = jnp.where(lo, s_lo, s_hi) * (1.0 / SW_DIM)
        return x * lax.rsqrt(ms + EPS) * gain

    kg = kg_ref[...]
    kk = jnp.concatenate([pair_norm(kp_ref[...], kg), pair_norm(kc_ref[...], kg)], axis=0)
    vv = jnp.concatenate([vp_ref[...], vc_ref[...]], axis=0)
    kk_sw = pltpu.roll(kk, SW_DIM, axis=1)
    vv_sw = pltpu.roll(vv, SW_DIM, axis=1)
    k_ver = [[jnp.where(lo, kk, 0.0).astype(BF16), jnp.where(lo, 0.0, kk_sw).astype(BF16)],
             [jnp.where(lo, kk_sw, 0.0).astype(BF16), jnp.where(lo, 0.0, kk).astype(BF16)]]
    v_ver = [[jnp.where(lo, vv, 0.0).astype(BF16), jnp.where(lo, 0.0, vv_sw).astype(BF16)],
             [jnp.where(lo, vv_sw, 0.0).astype(BF16), jnp.where(lo, 0.0, vv).astype(BF16)]]

    qi = lax.broadcasted_iota(jnp.int32, (W, 2 * W), 0)
    kj = lax.broadcasted_iota(jnp.int32, (W, 2 * W), 1)
    diff = W + qi - kj
    valid = (diff >= 0) & (diff < W) & ((n > 0) | (kj >= W))
    scale = SW_DIM ** -0.5
    group = SW_QHEADS // SW_KVHEADS
    qg = qg_ref[...]
    for pb in range(SW_QHEADS // 2):
        sl = slice(pb * LANES, (pb + 1) * LANES)
        qn = pair_norm(q_ref[:, sl].astype(F32), qg)
        out = jnp.zeros((W, LANES), F32)
        for half in range(2):
            head = 2 * pb + half
            hk = head // group
            qm = jnp.where(lo if half == 0 else jnp.logical_not(lo), qn, 0.0).astype(BF16)
            s = jnp.where(valid, _dot_nt(qm, k_ver[hk][half]) * scale, -jnp.inf)
            sink = sink_ref[head:head + 1, 0:1]
            mx = jnp.maximum(jnp.max(s, axis=1, keepdims=True), sink)
            p = jnp.exp(s - mx)
            p = p / (jnp.sum(p, axis=1, keepdims=True) + jnp.exp(sink - mx))
            out = out + _dot(p.astype(BF16), v_ver[hk][half])
        o_ref[:, sl] = out.astype(o_ref.dtype)


def swa_attention(proj, tail, q_norm, k_norm, sinks, seq):
    m = proj.shape[0]
    nc = seq // CHUNK
    qg = jnp.tile(q_norm.reshape(1, SW_DIM), (1, LANES // SW_DIM))
    kg = jnp.tile(k_norm.reshape(1, SW_DIM), (1, LANES // SW_DIM))
    sink_b = jnp.broadcast_to(sinks.reshape(SW_QHEADS, 1), (SW_QHEADS, LANES)).astype(F32)

    def prev(i):
        return jnp.maximum(i - 1, 0)

    return pl.pallas_call(
        functools.partial(_swa_body, n_chunks=nc),
        grid=(m // CHUNK,),
        in_specs=[pl.BlockSpec((CHUNK, BRANCH_W), lambda i: (i, CB_SW_Q)),
                  pl.BlockSpec((CHUNK, LANES), lambda i: (prev(i), 0)),
                  pl.BlockSpec((CHUNK, LANES), lambda i: (i, 0)),
                  pl.BlockSpec((CHUNK, LANES), lambda i: (prev(i), 1)),
                  pl.BlockSpec((CHUNK, LANES), lambda i: (i, 1)),
                  pl.BlockSpec((1, LANES), lambda i: (0, 0)),
                  pl.BlockSpec((1, LANES), lambda i: (0, 0)),
                  pl.BlockSpec((SW_QHEADS, LANES), lambda i: (0, 0))],
        out_specs=pl.BlockSpec((CHUNK, BRANCH_W), lambda i: (i, 0)),
        out_shape=jax.ShapeDtypeStruct((m, BRANCH_W), BF16),
        compiler_params=_cparams(("parallel",)),
    )(proj, tail, tail, tail, tail, qg, kg, sink_b)


def _ffn_gu_body(a_ref, bg_ref, bu_ref, o_ref):
    a = a_ref[...]
    g = _dot(a, bg_ref[...])
    u = _dot(a, bu_ref[...])
    o_ref[...] = (g * jax.nn.sigmoid(g) * u).astype(o_ref.dtype)


def ffn_gate_up(h, w_gu, tm=1024, tn=512):
    m, k = h.shape
    f = w_gu.shape[1] // 2
    nj = f // tn
    return pl.pallas_call(
        _ffn_gu_body,
        grid=(nj, m // tm),
        in_specs=[pl.BlockSpec((tm, k), lambda j, i: (i, 0)),
                  pl.BlockSpec((k, tn), lambda j, i: (0, j)),
                  pl.BlockSpec((k, tn), lambda j, i: (0, nj + j))],
        out_specs=pl.BlockSpec((tm, tn), lambda j, i: (i, j)),
        out_shape=jax.ShapeDtypeStruct((m, f), BF16),
        compiler_params=_cparams(("parallel", "parallel")),
    )(h, w_gu, w_gu)


def _ffn_down_body(a_ref, b_ref, x_ref, o_ref, acc_ref, *, nk):
    kk = pl.program_id(2)
    part = _dot(a_ref[...], b_ref[...])

    @pl.when(kk == 0)
    def _():
        acc_ref[...] = part

    @pl.when(kk > 0)
    def _():
        acc_ref[...] += part

    @pl.when(kk == nk - 1)
    def _():
        o_ref[...] = x_ref[...] + acc_ref[...]


def ffn_down(act, w_down, x, tm=1024, tn=1024, tk=2048):
    m, f = act.shape
    n = w_down.shape[1]
    nk = f // tk
    return pl.pallas_call(
        functools.partial(_ffn_down_body, nk=nk),
        grid=(n // tn, m // tm, nk),
        in_specs=[pl.BlockSpec((tm, tk), lambda j, i, k: (i, k)),
                  pl.BlockSpec((tk, tn), lambda j, i, k: (k, j)),
                  pl.BlockSpec((tm, tn), lambda j, i, k: (i, j))],
        out_specs=pl.BlockSpec((tm, tn), lambda j, i, k: (i, j)),
        out_shape=jax.ShapeDtypeStruct((m, n), F32),
        scratch_shapes=[pltpu.VMEM((tm, tn), F32)],
        compiler_params=_cparams(("parallel", "parallel", "arbitrary")),
    )(act, w_down, x)


MOE_TM = 512
ROW_SLABS = 32
DMA_WINDOW = 32


def _router_body(x_ref, g_ref, r_ref, h_ref, idx_ref, wts_ref):
    x = x_ref[...]
    ms = jnp.mean(x * x, axis=-1, keepdims=True)
    h = x * lax.rsqrt(ms + EPS) * g_ref[...]
    h_ref[...] = h.astype(h_ref.dtype)
    logits = _dot_f32(h, r_ref[...])
    lane = lax.broadcasted_iota(jnp.int32, logits.shape, 1)
    logits = jnp.where(lane < N_EXPERTS, logits, -jnp.inf)
    m1 = jnp.max(logits, axis=1, keepdims=True)
    i1 = jnp.min(jnp.where(logits == m1, lane, LANES), axis=1, keepdims=True)
    rest = jnp.where(lane == i1, -jnp.inf, logits)
    m2 = jnp.max(rest, axis=1, keepdims=True)
    i2 = jnp.min(jnp.where(rest == m2, lane, LANES), axis=1, keepdims=True)
    e2 = jnp.exp(m2 - m1)
    w1 = 1.0 / (1.0 + e2)
    w2 = e2 / (1.0 + e2)
    idx_ref[...] = jnp.where(lane == 0, i1, jnp.where(lane == 1, i2, 0))
    wts_ref[...] = jnp.where(lane == 0, w1, jnp.where(lane == 1, w2, 0.0))


def norm_and_route(x, g, router, tm=256):
    m, d = x.shape
    r_pad = jnp.zeros((d, LANES), F32).at[:, 0:N_EXPERTS].set(router)
    return pl.pallas_call(
        _router_body,
        grid=(m // tm,),
        in_specs=[pl.BlockSpec((tm, d), lambda i: (i, 0)), pl.BlockSpec((1, d), lambda i: (0, 0)),
                  pl.BlockSpec((d, LANES), lambda i: (0, 0))],
        out_specs=[pl.BlockSpec((tm, d), lambda i: (i, 0)), pl.BlockSpec((tm, LANES), lambda i: (i, 0)),
                   pl.BlockSpec((tm, LANES), lambda i: (i, 0))],
        out_shape=[jax.ShapeDtypeStruct((m, d), BF16), jax.ShapeDtypeStruct((m, LANES), jnp.int32),
                   jax.ShapeDtypeStruct((m, LANES), F32)],
        compiler_params=_cparams(("parallel",)),
    )(x, g.reshape(1, d), r_pad)


def _route_plan(idx, tm):
    m = idx.shape[0]
    n_tiles_max = -(-(2 * m + N_EXPERTS * (tm - 1)) // tm)
    e_flat = jnp.concatenate([idx[:, 0], idx[:, 1]])
    onehot = (e_flat[:, None] == jnp.arange(N_EXPERTS, dtype=jnp.int32)[None, :]).astype(jnp.int32)
    csum = jnp.cumsum(onehot, axis=0)
    counts = csum[-1]
    rank = jnp.sum((csum - onehot) * onehot, axis=1)
    padded = (counts + tm - 1) // tm * tm
    ends = jnp.cumsum(padded)
    starts = ends - padded
    dest = (jnp.sum(onehot * starts[None, :], axis=1) + rank).astype(jnp.int32)
    tile_row0 = jnp.arange(n_tiles_max, dtype=jnp.int32) * tm
    tile_expert = jnp.minimum(jnp.sum((tile_row0[:, None] >= ends[None, :]).astype(jnp.int32), axis=1),
                              N_EXPERTS - 1).astype(jnp.int32)
    n_tiles = (ends[-1] // tm).astype(jnp.int32).reshape(1)
    pad_rows = jnp.concatenate([starts + counts, ends]).astype(jnp.int32)
    return dest, tile_expert, n_tiles, pad_rows, n_tiles_max


def _windowed_row_copies(n, window, start, wait):
    first = jnp.minimum(n, window)
    lax.fori_loop(0, first, lambda r, c: (start(r), c)[1], 0)

    def steady(r, c):
        wait()
        start(r)
        return c

    lax.fori_loop(first, n, steady, 0)
    lax.fori_loop(0, first, lambda r, c: (wait(), c)[1], 0)


def _dispatch_body(dest_ref, pad_ref, h_ref, xs_ref, zero_ref, sem, *, m_tokens):
    def wait():
        pltpu.make_async_copy(h_ref.at[0], xs_ref.at[0], sem).wait()

    for k in range(2):
        def start(t, k=k):
            pltpu.make_async_copy(h_ref.at[t], xs_ref.at[dest_ref[k * m_tokens + t]], sem).start()
        _windowed_row_copies(m_tokens, DMA_WINDOW, start, wait)

    zero_ref[...] = jnp.zeros_like(zero_ref)
    for e in range(N_EXPERTS):
        lo = pad_ref[e]

        def zstart(r, lo=lo):
            pltpu.make_async_copy(zero_ref, xs_ref.at[lo + r], sem).start()
        _windowed_row_copies(pad_ref[N_EXPERTS + e] - lo, DMA_WINDOW, zstart, wait)


def moe_dispatch(h3, dest, pad_rows, n_rows):
    m = h3.shape[0]
    return pl.pallas_call(
        functools.partial(_dispatch_body, m_tokens=m),
        grid_spec=pltpu.PrefetchScalarGridSpec(
            num_scalar_prefetch=2, grid=(1,),
            in_specs=[pl.BlockSpec(memory_space=pl.ANY)],
            out_specs=pl.BlockSpec(memory_space=pl.ANY),
            scratch_shapes=[pltpu.VMEM(h3.shape[1:], h3.dtype), pltpu.SemaphoreType.DMA(())]),
        out_shape=jax.ShapeDtypeStruct((n_rows,) + h3.shape[1:], h3.dtype),
        compiler_params=_cparams(("arbitrary",)),
    )(dest, pad_rows, h3)


def _collect_body(pos_ref, ys_ref, yg_ref, sem, *, n_assign):
    def wait():
        pltpu.make_async_copy(ys_ref.at[0], yg_ref.at[0], sem).wait()

    def start(a):
        pltpu.make_async_copy(ys_ref.at[pos_ref[a]], yg_ref.at[a], sem).start()

    _windowed_row_copies(n_assign, DMA_WINDOW, start, wait)


def moe_collect(ys3, pos):
    n_assign = pos.shape[0]
    return pl.pallas_call(
        functools.partial(_collect_body, n_assign=n_assign),
        grid_spec=pltpu.PrefetchScalarGridSpec(
            num_scalar_prefetch=1, grid=(1,),
            in_specs=[pl.BlockSpec(memory_space=pl.ANY)],
            out_specs=pl.BlockSpec(memory_space=pl.ANY),
            scratch_shapes=[pltpu.SemaphoreType.DMA(())]),
        out_shape=jax.ShapeDtypeStruct((n_assign,) + ys3.shape[1:], ys3.dtype),
        compiler_params=_cparams(("arbitrary",)),
    )(pos, ys3)


def _live_tile(t, nt_ref):
    return jnp.minimum(t, nt_ref[0] - 1)


def _moe_gu_body(te_ref, nt_ref, a_ref, bg_ref, bu_ref, o_ref):
    t = pl.program_id(1)

    @pl.when(t < nt_ref[0])
    def _():
        a = a_ref[...]
        g = _dot(a, bg_ref[...])
        u = _dot(a, bu_ref[...])
        o_ref[...] = (g * jax.nn.sigmoid(g) * u).astype(o_ref.dtype)

    @pl.when(t >= nt_ref[0])
    def _():
        o_ref[...] = jnp.zeros_like(o_ref)


def moe_gate_up(xs, w_gu, tile_expert, n_tiles, tm, tn=512):
    rows, k = xs.shape
    f = w_gu.shape[2] // 2
    nj = f // tn
    return pl.pallas_call(
        _moe_gu_body,
        grid_spec=pltpu.PrefetchScalarGridSpec(
            num_scalar_prefetch=2, grid=(nj, rows // tm),
            in_specs=[pl.BlockSpec((tm, k), lambda j, t, te, nt: (_live_tile(t, nt), 0)),
                      pl.BlockSpec((None, k, tn), lambda j, t, te, nt: (te[_live_tile(t, nt)], 0, j)),
                      pl.BlockSpec((None, k, tn), lambda j, t, te, nt: (te[_live_tile(t, nt)], 0, nj + j))],
            out_specs=pl.BlockSpec((tm, tn), lambda j, t, te, nt: (t, j))),
        out_shape=jax.ShapeDtypeStruct((rows, f), BF16),
        compiler_params=_cparams(("arbitrary", "arbitrary")),
    )(tile_expert, n_tiles, xs, w_gu, w_gu)


def _moe_down_body(te_ref, nt_ref, a_ref, b_ref, o_ref):
    t = pl.program_id(1)

    @pl.when(t < nt_ref[0])
    def _():
        o_ref[...] = _dot(a_ref[...], b_ref[...]).astype(o_ref.dtype)

    @pl.when(t >= nt_ref[0])
    def _():
        o_ref[...] = jnp.zeros_like(o_ref)


def moe_down(act, w_down, tile_expert, n_tiles, tm, tn=1024):
    rows, f = act.shape
    n = w_down.shape[2]
    return pl.pallas_call(
        _moe_down_body,
        grid_spec=pltpu.PrefetchScalarGridSpec(
            num_scalar_prefetch=2, grid=(n // tn, rows // tm),
            in_specs=[pl.BlockSpec((tm, f), lambda j, t, te, nt: (_live_tile(t, nt), 0)),
                      pl.BlockSpec((None, f, tn), lambda j, t, te, nt: (te[_live_tile(t, nt)], 0, j))],
            out_specs=pl.BlockSpec((tm, tn), lambda j, t, te, nt: (t, j))),
        out_shape=jax.ShapeDtypeStruct((rows, n), BF16),
        compiler_params=_cparams(("arbitrary", "arbitrary")),
    )(tile_expert, n_tiles, act, w_down)


def _combine_body(x_ref, y1_ref, y2_ref, w_ref, o_ref):
    w = w_ref[...]
    o_ref[...] = (x_ref[...] + _lane_col(w, 0) * y1_ref[...].astype(F32)
                  + _lane_col(w, 1) * y2_ref[...].astype(F32))


def moe_combine(x, yg, wts, tm=256):
    m, d = x.shape
    nb = m // tm
    return pl.pallas_call(
        _combine_body,
        grid=(nb,),
        in_specs=[pl.BlockSpec((tm, d), lambda i: (i, 0)), pl.BlockSpec((tm, d), lambda i: (i, 0)),
                  pl.BlockSpec((tm, d), lambda i: (nb + i, 0)), pl.BlockSpec((tm, LANES), lambda i: (i, 0))],
        out_specs=pl.BlockSpec((tm, d), lambda i: (i, 0)),
        out_shape=jax.ShapeDtypeStruct((m, d), F32),
        compiler_params=_cparams(("parallel",)),
    )(x, yg, yg, wts)


def moe_ffn(x, g, router, w_gu, w_down, tm=MOE_TM):
    m, d = x.shape
    h, idx, wts = norm_and_route(x, g, router)
    dest, tile_expert, n_tiles, pad_rows, n_tiles_max = _route_plan(idx, tm)
    rows = n_tiles_max * tm
    xs3 = moe_dispatch(h.reshape(m, ROW_SLABS, d // ROW_SLABS), dest, pad_rows, rows)
    act = moe_gate_up(xs3.reshape(rows, d), w_gu, tile_expert, n_tiles, tm)
    ys = moe_down(act, w_down, tile_expert, n_tiles, tm)
    yg3 = moe_collect(ys.reshape(rows, ROW_SLABS, d // ROW_SLABS), dest)
    return moe_combine(x, yg3.reshape(2 * m, d), wts)


def _split_w_in(w_in):
    a_w = BRANCH_W
    o_ml = 3 * a_w
    o_if = o_ml + 4 * a_w
    o_sg = o_if + 2 * N_HEADS
    o_swq = o_sg + 2 * a_w
    o_swk = o_swq + SW_QHEADS * SW_DIM
    o_gate = o_swk + 2 * SW_KVHEADS * SW_DIM
    main = jnp.concatenate([w_in[:, :o_if], w_in[:, o_sg:o_swk], w_in[:, o_gate:]], axis=1).astype(BF16)
    pad = jnp.zeros((w_in.shape[0], LANES - 2 * N_HEADS), w_in.dtype)
    tail = jnp.concatenate([w_in[:, o_swk:o_gate], w_in[:, o_if:o_sg], pad], axis=1).astype(BF16)
    return main, tail


def _mixer(h, layer_params, batch, seq):
    (w_in, conv_w, conv_b, gate_b, head_norm, sg_w, sg_b, q_norm, k_norm, sinks, w_branch, w_out) = layer_params
    w_main, w_tail = _split_w_in(w_in)
    proj = inproj_main(h, w_main)
    tail = inproj_tail(h, w_tail)
    y_a = sb_attention(proj, batch, seq)
    y_b = mlstm(proj, tail, conv_w, conv_b, gate_b, head_norm, batch, seq)
    y_c = spatial_gating(proj, sg_w, sg_b)
    y_d = swa_attention(proj, tail, q_norm, k_norm, sinks, seq)
    merged = gated_merge((y_a, y_b, y_c, y_d), w_branch.astype(BF16), proj)
    return merged, w_out.astype(BF16)


def kernel(x, norm_mix, w_in, ml_conv_w, ml_conv_b, ml_gate_b, ml_head_norm, sg_w, sg_b, sw_q_norm, sw_k_norm,
           sw_sinks, w_branch, w_out, norm_ffn, ffn_w_gu, ffn_w_down, moe_router, moe_w_gu, moe_w_down):
    batch, seq, d = x.shape
    depth = norm_mix.shape[0]
    xf = x.reshape(batch * seq, d)
    for layer in range(depth):
        h = rmsnorm(xf, norm_mix[layer])
        merged, w_o = _mixer(h, (w_in[layer], ml_conv_w[layer], ml_conv_b[layer], ml_gate_b[layer],
                                 ml_head_norm[layer], sg_w[layer], sg_b[layer], sw_q_norm[layer],
                                 sw_k_norm[layer], sw_sinks[layer], w_branch[layer], w_out[layer]),
                             batch, seq)
        xf = matmul_residual(merged, w_o, xf)
        j = layer // 2
        if layer % 2 == 0:
            h = rmsnorm(xf, norm_ffn[layer])
            act = ffn_gate_up(h, ffn_w_gu[j].astype(BF16))
            xf = ffn_down(act, ffn_w_down[j].astype(BF16), xf)
        else:
            xf = moe_ffn(xf, norm_ffn[layer], moe_router[j], moe_w_gu[j].astype(BF16), moe_w_down[j].astype(BF16))
    return xf.reshape(batch, seq, d)
```

```python
import functools

import jax
import jax.numpy as jnp
from jax import lax
from jax.experimental import pallas as pl
from jax.experimental.pallas import tpu as pltpu

F32 = jnp.float32
BF16 = jnp.bfloat16
EPS = 1e-6
LANES = 128
CHUNK = 128
VMEM_LIMIT_MB = 56

D_MODEL = 4096
N_HEADS = 8
HEAD_DIM = 128
BRANCH_W = N_HEADS * HEAD_DIM
SW_QHEADS, SW_KVHEADS, SW_DIM = 16, 2, 64
N_BRANCH = 4
N_EXPERTS = 8

CB_SB_Q, CB_SB_K, CB_SB_V = 0, 1, 2
CB_ML_Q, CB_ML_K, CB_ML_V, CB_ML_O = 3, 4, 5, 6
CB_SG_U, CB_SG_V = 7, 8
CB_SW_Q = 9
CB_GATE0 = 10
N_MAIN = (CB_GATE0 + N_BRANCH * D_MODEL // BRANCH_W) * BRANCH_W
N_TAIL = 3 * LANES


def _cparams(sem, vmem_mb=VMEM_LIMIT_MB):
    return pltpu.CompilerParams(dimension_semantics=sem, vmem_limit_bytes=vmem_mb * 1024 * 1024)


def _dot(a, b):
    return jnp.dot(a, b, preferred_element_type=F32)


def _dot_nt(a, b):
    return lax.dot_general(a, b, (((1,), (1,)), ((), ())), preferred_element_type=F32)


def _dot_tn(a, b):
    return lax.dot_general(a, b, (((0,), (0,)), ((), ())), preferred_element_type=F32)


def _dot_f32(a, b):
    return jnp.dot(a, b, preferred_element_type=F32, precision=lax.Precision.HIGHEST)


def _lane_col(x, idx):
    lane = lax.broadcasted_iota(jnp.int32, x.shape, 1)
    return jnp.sum(jnp.where(lane == idx, x, 0.0), axis=1, keepdims=True)


def _softplus(z):
    return jnp.maximum(z, 0.0) + jnp.log(1.0 + jnp.exp(-jnp.abs(z)))


def _rmsnorm_body(x_ref, g_ref, o_ref):
    x = x_ref[...]
    ms = jnp.mean(x * x, axis=-1, keepdims=True)
    o_ref[...] = (x * lax.rsqrt(ms + EPS) * g_ref[...]).astype(o_ref.dtype)


def rmsnorm(x, g, tm=256):
    m, d = x.shape
    return pl.pallas_call(
        _rmsnorm_body,
        grid=(m // tm,),
        in_specs=[pl.BlockSpec((tm, d), lambda i: (i, 0)), pl.BlockSpec((1, d), lambda i: (0, 0))],
        out_specs=pl.BlockSpec((tm, d), lambda i: (i, 0)),
        out_shape=jax.ShapeDtypeStruct((m, d), BF16),
        compiler_params=_cparams(("parallel",)),
    )(x, g.reshape(1, d))


def _mm_body(a_ref, b_ref, o_ref):
    o_ref[...] = _dot(a_ref[...], b_ref[...]).astype(o_ref.dtype)


def inproj_main(h, w, tm=1024, tn=1024):
    m, k = h.shape
    n = w.shape[1]
    return pl.pallas_call(
        _mm_body,
        grid=(n // tn, m // tm),
        in_specs=[pl.BlockSpec((tm, k), lambda j, i: (i, 0)), pl.BlockSpec((k, tn), lambda j, i: (0, j))],
        out_specs=pl.BlockSpec((tm, tn), lambda j, i: (i, j)),
        out_shape=jax.ShapeDtypeStruct((m, n), BF16),
        compiler_params=_cparams(("parallel", "parallel")),
    )(h, w)


def inproj_tail(h, w, tm=1024):
    m, k = h.shape
    n = w.shape[1]
    return pl.pallas_call(
        _mm_body,
        grid=(m // tm,),
        in_specs=[pl.BlockSpec((tm, k), lambda i: (i, 0)), pl.BlockSpec((k, n), lambda i: (0, 0))],
        out_specs=pl.BlockSpec((tm, n), lambda i: (i, 0)),
        out_shape=jax.ShapeDtypeStruct((m, n), F32),
        compiler_params=_cparams(("parallel",)),
    )(h, w)


def _mm_res_body(a_ref, b_ref, x_ref, o_ref):
    o_ref[...] = x_ref[...] + _dot(a_ref[...], b_ref[...])


def matmul_residual(a, w, x, tm=512, tn=1024):
    m, k = a.shape
    n = w.shape[1]
    return pl.pallas_call(
        _mm_res_body,
        grid=(n // tn, m // tm),
        in_specs=[pl.BlockSpec((tm, k), lambda j, i: (i, 0)), pl.BlockSpec((k, tn), lambda j, i: (0, j)),
                  pl.BlockSpec((tm, tn), lambda j, i: (i, j))],
        out_specs=pl.BlockSpec((tm, tn), lambda j, i: (i, j)),
        out_shape=jax.ShapeDtypeStruct((m, n), F32),
        compiler_params=_cparams(("parallel", "parallel")),
    )(a, w, x)


def _merge_body(ya_ref, yb_ref, yc_ref, yd_ref, wa_ref, wb_ref, wc_ref, wd_ref,
                ga_ref, gb_ref, gc_ref, gd_ref, o_ref):
    def gate(g_ref):
        return jax.nn.sigmoid(g_ref[...].astype(F32))

    acc = gate(ga_ref) * _dot(ya_ref[...], wa_ref[...])
    acc = acc + gate(gb_ref) * _dot(yb_ref[...], wb_ref[...])
    acc = acc + gate(gc_ref) * _dot(yc_ref[...], wc_ref[...])
    acc = acc + gate(gd_ref) * _dot(yd_ref[...], wd_ref[...])
    o_ref[...] = acc.astype(o_ref.dtype)


def gated_merge(ys, w_branch, proj, tm=512, tn=1024):
    m, kb = ys[0].shape
    d = w_branch.shape[2]
    gpb = d // tn
    g0 = CB_GATE0 * BRANCH_W // tn
    y_specs = [pl.BlockSpec((tm, kb), lambda j, i: (i, 0)) for _ in range(N_BRANCH)]
    w_specs = [pl.BlockSpec((None, kb, tn), functools.partial(lambda j, i, br: (br, 0, j), br=br))
               for br in range(N_BRANCH)]
    g_specs = [pl.BlockSpec((tm, tn), functools.partial(lambda j, i, br: (i, g0 + br * gpb + j), br=br))
               for br in range(N_BRANCH)]
    return pl.pallas_call(
        _merge_body,
        grid=(d // tn, m // tm),
        in_specs=y_specs + w_specs + g_specs,
        out_specs=pl.BlockSpec((tm, tn), lambda j, i: (i, j)),
        out_shape=jax.ShapeDtypeStruct((m, d), BF16),
        compiler_params=_cparams(("parallel", "parallel")),
    )(*ys, w_branch, w_branch, w_branch, w_branch, proj, proj, proj, proj)


def _sb_body(q_ref, k_ref, v_ref, o_ref, *, tq, tk, scale):
    qi = pl.program_id(2)
    r = tq // tk
    q = q_ref[...]
    ur = lax.broadcasted_iota(jnp.int32, (tk, 2 * tk), 0)
    uc = lax.broadcasted_iota(jnp.int32, (tk, 2 * tk), 1)
    cum_w = jnp.where((uc >= tk) | (ur > uc), 1.0, 0.0).astype(BF16)

    def key_block(start, acc, c, mask):
        k = k_ref[pl.ds(start, tq), :]
        v = v_ref[pl.ds(start, tq), :]
        z = _dot_nt(q, k) * scale
        nz = -z
        lg = jnp.log(1.0 + jnp.exp(jnp.minimum(z, nz)))
        l1mb = jnp.minimum(nz, 0.0) - lg
        log_beta = l1mb + z
        if mask is not None:
            l1mb = jnp.where(mask, l1mb, 0.0)
        l1mb_b = l1mb.astype(BF16)
        ws = [None] * r
        for j in range(r - 1, -1, -1):
            sl = slice(j * tk, (j + 1) * tk)
            t = _dot(l1mb_b[:, sl], cum_w)
            w = jnp.exp(log_beta[:, sl] + t[:, :tk] + c)
            if mask is not None:
                w = jnp.where(mask[:, sl], w, 0.0)
            ws[j] = w.astype(BF16)
            c = c + t[:, tk:]
        acc = acc + _dot(jnp.concatenate(ws, axis=1), v)
        return acc, c

    acc = jnp.zeros((tq, HEAD_DIM), F32)
    c = jnp.zeros((tq, tk), F32)
    rows = lax.broadcasted_iota(jnp.int32, (tq, tq), 0)
    cols = lax.broadcasted_iota(jnp.int32, (tq, tq), 1)
    acc, c = key_block(pl.multiple_of(qi * tq, tq), acc, c, cols < rows)

    def body(it, carry):
        return key_block(pl.multiple_of((qi - 1 - it) * tq, tq), *carry, None)

    acc, c = lax.fori_loop(0, qi, body, (acc, c))
    o_ref[...] = acc.astype(o_ref.dtype)


def sb_attention(proj, batch, seq, tq=512, tk=128):
    m = proj.shape[0]
    nq = seq // tq
    return pl.pallas_call(
        functools.partial(_sb_body, tq=tq, tk=tk, scale=HEAD_DIM ** -0.5),
        grid=(batch, N_HEADS, nq),
        in_specs=[pl.BlockSpec((tq, HEAD_DIM), lambda b, h, i: (b * nq + i, CB_SB_Q * N_HEADS + h)),
                  pl.BlockSpec((seq, HEAD_DIM), lambda b, h, i: (b, CB_SB_K * N_HEADS + h)),
                  pl.BlockSpec((seq, HEAD_DIM), lambda b, h, i: (b, CB_SB_V * N_HEADS + h))],
        out_specs=pl.BlockSpec((tq, HEAD_DIM), lambda b, h, i: (b * nq + i, h)),
        out_shape=jax.ShapeDtypeStruct((m, BRANCH_W), BF16),
        compiler_params=_cparams(("parallel", "parallel", "arbitrary")),
    )(proj, proj, proj)


def _mlstm_body(q_ref, k_ref, v_ref, o_ref, g_ref, cw_ref, cb_ref, gb_ref, hn_ref, y_ref,
                xs_ref, c_ref, n_ref, m_ref, *, n_conv):
    L = CHUNK
    chunk = pl.program_id(1)

    @pl.when(chunk == 0)
    def _():
        xs_ref[0:8, :] = jnp.zeros((8, 2 * BRANCH_W), F32)
        c_ref[...] = jnp.zeros_like(c_ref)
        n_ref[...] = jnp.zeros_like(n_ref)
        m_ref[...] = jnp.zeros_like(m_ref)

    xs_ref[8:8 + L, 0:BRANCH_W] = q_ref[...].astype(F32)
    xs_ref[8:8 + L, BRANCH_W:2 * BRANCH_W] = k_ref[...].astype(F32)
    conv = jnp.zeros((L, 2 * BRANCH_W), F32) + cb_ref[...]
    for j in range(n_conv):
        conv = conv + cw_ref[j:j + 1, :] * xs_ref[pl.ds(8 - (n_conv - 1) + j, L), :]
    xs_ref[0:8, :] = xs_ref[L:L + 8, :]
    qk = conv * jax.nn.sigmoid(conv)
    qf = qk[:, 0:BRANCH_W]
    kf = qk[:, BRANCH_W:2 * BRANCH_W] * (HEAD_DIM ** -0.5)

    g = g_ref[...] + gb_ref[...]
    lf = -_softplus(-g)
    ri = lax.broadcasted_iota(jnp.int32, (L, L), 0)
    ci = lax.broadcasted_iota(jnp.int32, (L, L), 1)
    tril = ri >= ci
    tril_f = jnp.where(tril, 1.0, 0.0).astype(F32)
    triu_f = jnp.where(ri <= ci, 1.0, 0.0).astype(F32)
    b_cols = _dot_f32(tril_f, lf)
    g_t = g.T
    b_rows = _dot_f32(lf.T, triu_f)

    for h in range(N_HEADS):
        sl = slice(h * HEAD_DIM, (h + 1) * HEAD_DIM)
        qh = qf[:, sl]
        kh = kf[:, sl]
        qb = qh.astype(BF16)
        kb = kh.astype(BF16)
        vb = v_ref[:, sl]
        i_col = _lane_col(g, h)
        b_col = _lane_col(b_cols, N_HEADS + h)
        i_row = g_t[h:h + 1, :]
        b_row = b_rows[N_HEADS + h:N_HEADS + h + 1, :]
        m_prev = m_ref[h:h + 1, 0:1]
        c_prev = c_ref[h]
        n_prev = n_ref[h:h + 1, :]

        log_d = jnp.where(tril, b_col - b_row + i_row, -jnp.inf)
        m_inter = b_col + m_prev
        m_t = jnp.maximum(m_inter, jnp.max(log_d, axis=1, keepdims=True))
        dmat = jnp.exp(log_d - m_t)
        s = _dot_nt(qb, kb) * dmat
        inter = jnp.exp(m_inter - m_t)
        num = _dot(s.astype(BF16), vb) + inter * _dot(qb, c_prev.astype(BF16))
        den = jnp.sum(s, axis=1, keepdims=True) + inter * jnp.sum(qb.astype(F32) * n_prev, axis=1, keepdims=True)
        hh = num / jnp.maximum(jnp.abs(den), jnp.exp(-m_t))

        m_new = m_t[L - 1:L, :]
        b_last = b_col[L - 1:L, :]
        w_col = jnp.exp(b_last - b_col + i_col - m_new)
        decay = jnp.exp(b_last + m_prev - m_new)
        kw = kh * w_col
        c_ref[h] = decay * c_prev + _dot_tn(kw.astype(BF16), vb)
        n_ref[h:h + 1, :] = decay * n_prev + jnp.sum(kw, axis=0, keepdims=True)
        m_ref[h:h + 1, :] = jnp.broadcast_to(m_new, (1, LANES))

        hn = hh * lax.rsqrt(jnp.mean(hh * hh, axis=1, keepdims=True) + EPS) * hn_ref[:, sl]
        y_ref[:, sl] = (jax.nn.sigmoid(o_ref[:, sl].astype(F32)) * hn).astype(y_ref.dtype)


def mlstm(proj, tail, conv_w, conv_b, gate_b, head_norm, batch, seq):
    m = proj.shape[0]
    nc = seq // CHUNK
    n_conv = conv_w.shape[0]
    gb = jnp.zeros((1, LANES), F32).at[0, 0:2 * N_HEADS].set(gate_b.reshape(-1))

    def blk(cb):
        return pl.BlockSpec((CHUNK, BRANCH_W), lambda b, c: (b * nc + c, cb))

    def const(shape):
        return pl.BlockSpec(shape, lambda b, c: (0, 0))

    return pl.pallas_call(
        functools.partial(_mlstm_body, n_conv=n_conv),
        grid=(batch, nc),
        in_specs=[blk(CB_ML_Q), blk(CB_ML_K), blk(CB_ML_V), blk(CB_ML_O),
                  pl.BlockSpec((CHUNK, LANES), lambda b, c: (b * nc + c, 2)),
                  const((n_conv, 2 * BRANCH_W)), const((1, 2 * BRANCH_W)), const((1, LANES)),
                  const((1, BRANCH_W))],
        out_specs=pl.BlockSpec((CHUNK, BRANCH_W), lambda b, c: (b * nc + c, 0)),
        out_shape=jax.ShapeDtypeStruct((m, BRANCH_W), BF16),
        scratch_shapes=[pltpu.VMEM((CHUNK + 8, 2 * BRANCH_W), F32),
                        pltpu.VMEM((N_HEADS, HEAD_DIM, HEAD_DIM), F32),
                        pltpu.VMEM((N_HEADS, HEAD_DIM), F32),
                        pltpu.VMEM((N_HEADS, LANES), F32)],
        compiler_params=_cparams(("parallel", "arbitrary")),
    )(proj, proj, proj, proj, tail, conv_w, conv_b.reshape(1, -1), gb, head_norm.reshape(1, -1))


def _gelu(x):
    return 0.5 * x * (1.0 + lax.erf(x * (2.0 ** -0.5)))


def _sgu_body(u_ref, v_ref, w_ref, b_ref, o_ref):
    L = CHUNK
    u = _gelu(u_ref[...].astype(F32))
    v = _gelu(v_ref[...].astype(F32))
    mu = jnp.mean(v, axis=1, keepdims=True)
    vc = v - mu
    var = jnp.mean(vc * vc, axis=1, keepdims=True)
    vn = (vc * lax.rsqrt(var + EPS)).astype(BF16)
    ri = lax.broadcasted_iota(jnp.int32, (L, L), 0)
    ci = lax.broadcasted_iota(jnp.int32, (L, L), 1)
    tril = ri >= ci
    bias = b_ref[...]
    for g in range(N_HEADS):
        sl = slice(g * HEAD_DIM, (g + 1) * HEAD_DIM)
        wg = jnp.where(tril, w_ref[g], 0.0).astype(BF16)
        mixed = _dot(wg, vn[:, sl]) + _lane_col(bias, g)
        o_ref[:, sl] = (u[:, sl] * mixed).astype(o_ref.dtype)


def spatial_gating(proj, sg_w, sg_b):
    m = proj.shape[0]
    bias = jnp.zeros((CHUNK, LANES), F32).at[:, 0:N_HEADS].set(sg_b.T)
    return pl.pallas_call(
        _sgu_body,
        grid=(m // CHUNK,),
        in_specs=[pl.BlockSpec((CHUNK, BRANCH_W), lambda i: (i, CB_SG_U)),
                  pl.BlockSpec((CHUNK, BRANCH_W), lambda i: (i, CB_SG_V)),
                  pl.BlockSpec((N_HEADS, CHUNK, CHUNK), lambda i: (0, 0, 0)),
                  pl.BlockSpec((CHUNK, LANES), lambda i: (0, 0))],
        out_specs=pl.BlockSpec((CHUNK, BRANCH_W), lambda i: (i, 0)),
        out_shape=jax.ShapeDtypeStruct((m, BRANCH_W), BF16),
        compiler_params=_cparams(("parallel",)),
    )(proj, proj, sg_w, bias)


def _swa_body(q_ref, kp_ref, kc_ref, vp_ref, vc_ref, qg_ref, kg_ref, sink_ref, o_ref, *, n_chunks):
    W = CHUNK
    n = pl.program_id(0) % n_chunks
    lane = lax.broadcasted_iota(jnp.int32, (1, LANES), 1)
    lo = lane < SW_DIM

    def pair_norm(x, gain):
        x2 = x * x
        s_lo = jnp.sum(jnp.where(lo, x2, 0.0), axis=1, keepdims=True)
        s_hi = jnp.sum(jnp.where(lo, 0.0, x2), axis=1, keepdims=True)
        ms = jnp.where(lo, s_lo, s_hi) * (1.0 / SW_DIM)
        return x * lax.rsqrt(ms + EPS) * gain

    kg = kg_ref[...]
    kk = jnp.concatenate([pair_norm(kp_ref[...], kg), pair_norm(kc_ref[...], kg)], axis=0)
    vv = jnp.concatenate([vp_ref[...], vc_ref[...]], axis=0)
    kk_sw = pltpu.roll(kk, SW_DIM, axis=1)
    vv_sw = pltpu.roll(vv, SW_DIM, axis=1)
    k_ver = [[jnp.where(lo, kk, 0.0).astype(BF16), jnp.where(lo, 0.0, kk_sw).astype(BF16)],
             [jnp.where(lo, kk_sw, 0.0).astype(BF16), jnp.where(lo, 0.0, kk).astype(BF16)]]
    v_ver = [[jnp.where(lo, vv, 0.0).astype(BF16), jnp.where(lo, 0.0, vv_sw).astype(BF16)],
             [jnp.where(lo, vv_sw, 0.0).astype(BF16), jnp.where(lo, 0.0, vv).astype(BF16)]]

    stack = SW_QHEADS // SW_KVHEADS // 2
    qi = lax.broadcasted_iota(jnp.int32, (stack * W, 2 * W), 0) % W
    kj = lax.broadcasted_iota(jnp.int32, (stack * W, 2 * W), 1)
    diff = W + qi - kj
    valid = (diff >= 0) & (diff < W) & ((n > 0) | (kj >= W))
    scale = SW_DIM ** -0.5
    qg = qg_ref[...]
    qn = [pair_norm(q_ref[:, pb * LANES:(pb + 1) * LANES].astype(F32), qg) for pb in range(SW_QHEADS // 2)]
    for hk in range(SW_KVHEADS):
        outs = []
        for half in range(2):
            sel = lo if half == 0 else jnp.logical_not(lo)
            qs = jnp.concatenate([jnp.where(sel, qn[hk * stack + i], 0.0) for i in range(stack)], axis=0)
            s = jnp.where(valid, _dot_nt(qs.astype(BF16), k_ver[hk][half]) * scale, -jnp.inf)
            st = hk * 2 + half
            sink = sink_ref[st * stack * W:(st + 1) * stack * W, :]
            mx = jnp.maximum(jnp.max(s, axis=1, keepdims=True), sink)
            p = jnp.exp(s - mx)
            inv = 1.0 / (jnp.sum(p, axis=1, keepdims=True) + jnp.exp(sink - mx))
            outs.append(_dot(p.astype(BF16), v_ver[hk][half]) * inv)
        for i in range(stack):
            pb = hk * stack + i
            o_ref[:, pb * LANES:(pb + 1) * LANES] = (outs[0][i * W:(i + 1) * W]
                                                     + outs[1][i * W:(i + 1) * W]).astype(o_ref.dtype)


def swa_attention(proj, tail, q_norm, k_norm, sinks, seq):
    m = proj.shape[0]
    nc = seq // CHUNK
    qg = jnp.tile(q_norm.reshape(1, SW_DIM), (1, LANES // SW_DIM))
    kg = jnp.tile(k_norm.reshape(1, SW_DIM), (1, LANES // SW_DIM))
    stack = SW_QHEADS // SW_KVHEADS // 2
    order = [2 * (hk * stack + i) + half for hk in range(SW_KVHEADS) for half in range(2) for i in range(stack)]
    sink_rows = jnp.repeat(sinks.astype(F32)[jnp.array(order)], CHUNK).reshape(SW_QHEADS * CHUNK, 1)

    def prev(i):
        return jnp.maximum(i - 1, 0)

    return pl.pallas_call(
        functools.partial(_swa_body, n_chunks=nc),
        grid=(m // CHUNK,),
        in_specs=[pl.BlockSpec((CHUNK, BRANCH_W), lambda i: (i, CB_SW_Q)),
                  pl.BlockSpec((CHUNK, LANES), lambda i: (prev(i), 0)),
                  pl.BlockSpec((CHUNK, LANES), lambda i: (i, 0)),
                  pl.BlockSpec((CHUNK, LANES), lambda i: (prev(i), 1)),
                  pl.BlockSpec((CHUNK, LANES), lambda i: (i, 1)),
                  pl.BlockSpec((1, LANES), lambda i: (0, 0)),
                  pl.BlockSpec((1, LANES), lambda i: (0, 0)),
                  pl.BlockSpec((SW_QHEADS * CHUNK, 1), lambda i: (0, 0))],
        out_specs=pl.BlockSpec((CHUNK, BRANCH_W), lambda i: (i, 0)),
        out_shape=jax.ShapeDtypeStruct((m, BRANCH_W), BF16),
        compiler_params=_cparams(("parallel",)),
    )(proj, tail, tail, tail, tail, qg, kg, sink_rows)


def _ffn_gu_body(a_ref, bg_ref, bu_ref, o_ref):
    a = a_ref[...]
    g = _dot(a, bg_ref[...])
    u = _dot(a, bu_ref[...])
    o_ref[...] = (g * jax.nn.sigmoid(g) * u).astype(o_ref.dtype)


def ffn_gate_up(h, w_gu, tm=1024, tn=512):
    m, k = h.shape
    f = w_gu.shape[1] // 2
    nj = f // tn
    return pl.pallas_call(
        _ffn_gu_body,
        grid=(nj, m // tm),
        in_specs=[pl.BlockSpec((tm, k), lambda j, i: (i, 0)),
                  pl.BlockSpec((k, tn), lambda j, i: (0, j)),
                  pl.BlockSpec((k, tn), lambda j, i: (0, nj + j))],
        out_specs=pl.BlockSpec((tm, tn), lambda j, i: (i, j)),
        out_shape=jax.ShapeDtypeStruct((m, f), BF16),
        compiler_params=_cparams(("parallel", "parallel")),
    )(h, w_gu, w_gu)


def _ffn_down_body(a_ref, b_ref, x_ref, o_ref):
    kk = pl.program_id(2)
    part = _dot(a_ref[...], b_ref[...])

    @pl.when(kk == 0)
    def _():
        o_ref[...] = x_ref[...] + part

    @pl.when(kk > 0)
    def _():
        o_ref[...] += part


def ffn_down(act, w_down, x, tm=1024, tn=1024, tk=3584):
    m, f = act.shape
    n = w_down.shape[1]
    nk = f // tk
    return pl.pallas_call(
        _ffn_down_body,
        grid=(n // tn, m // tm, nk),
        in_specs=[pl.BlockSpec((tm, tk), lambda j, i, k: (i, k)),
                  pl.BlockSpec((tk, tn), lambda j, i, k: (k, j)),
                  pl.BlockSpec((tm, tn), lambda j, i, k: (i, j))],
        out_specs=pl.BlockSpec((tm, tn), lambda j, i, k: (i, j)),
        out_shape=jax.ShapeDtypeStruct((m, n), F32),
        compiler_params=_cparams(("parallel", "parallel", "arbitrary")),
    )(act, w_down, x)


MOE_TM = 512
ROW_SLABS = 32


def _router_body(x_ref, g_ref, r_ref, h_ref, idx_ref, wts_ref):
    x = x_ref[...]
    ms = jnp.mean(x * x, axis=-1, keepdims=True)
    h = x * lax.rsqrt(ms + EPS) * g_ref[...]
    h_ref[...] = h.astype(h_ref.dtype)
    logits = _dot_f32(h, r_ref[...])
    lane = lax.broadcasted_iota(jnp.int32, logits.shape, 1)
    logits = jnp.where(lane < N_EXPERTS, logits, -jnp.inf)
    m1 = jnp.max(logits, axis=1, keepdims=True)
    i1 = jnp.min(jnp.where(logits == m1, lane, LANES), axis=1, keepdims=True)
    rest = jnp.where(lane == i1, -jnp.inf, logits)
    m2 = jnp.max(rest, axis=1, keepdims=True)
    i2 = jnp.min(jnp.where(rest == m2, lane, LANES), axis=1, keepdims=True)
    e2 = jnp.exp(m2 - m1)
    w1 = 1.0 / (1.0 + e2)
    w2 = e2 / (1.0 + e2)
    idx_ref[...] = jnp.where(lane == 0, i1, jnp.where(lane == 1, i2, 0))
    wts_ref[...] = jnp.where(lane == 0, w1, jnp.where(lane == 1, w2, 0.0))


def norm_and_route(x, g, router, tm=256):
    m, d = x.shape
    r_pad = jnp.zeros((d, LANES), F32).at[:, 0:N_EXPERTS].set(router)
    return pl.pallas_call(
        _router_body,
        grid=(m // tm,),
        in_specs=[pl.BlockSpec((tm, d), lambda i: (i, 0)), pl.BlockSpec((1, d), lambda i: (0, 0)),
                  pl.BlockSpec((d, LANES), lambda i: (0, 0))],
        out_specs=[pl.BlockSpec((tm, d), lambda i: (i, 0)), pl.BlockSpec((tm, LANES), lambda i: (i, 0)),
                   pl.BlockSpec((tm, LANES), lambda i: (i, 0))],
        out_shape=[jax.ShapeDtypeStruct((m, d), BF16), jax.ShapeDtypeStruct((m, LANES), jnp.int32),
                   jax.ShapeDtypeStruct((m, LANES), F32)],
        compiler_params=_cparams(("parallel",)),
    )(x, g.reshape(1, d), r_pad)


def _route_plan(idx, tm):
    m = idx.shape[0]
    n_tiles_max = -(-(2 * m + N_EXPERTS * (tm - 1)) // tm)
    e_flat = jnp.concatenate([idx[:, 0], idx[:, 1]])
    onehot = (e_flat[:, None] == jnp.arange(N_EXPERTS, dtype=jnp.int32)[None, :]).astype(jnp.int32)
    csum = jnp.cumsum(onehot, axis=0)
    counts = csum[-1]
    rank = jnp.sum((csum - onehot) * onehot, axis=1)
    padded = (counts + tm - 1) // tm * tm
    ends = jnp.cumsum(padded)
    starts = ends - padded
    dest = (jnp.sum(onehot * starts[None, :], axis=1) + rank).astype(jnp.int32)
    tile_row0 = jnp.arange(n_tiles_max, dtype=jnp.int32) * tm
    tile_expert = jnp.minimum(jnp.sum((tile_row0[:, None] >= ends[None, :]).astype(jnp.int32), axis=1),
                              N_EXPERTS - 1).astype(jnp.int32)
    n_tiles = (ends[-1] // tm).astype(jnp.int32).reshape(1)
    token = jnp.arange(2 * m, dtype=jnp.int32) % m
    row_token = jnp.zeros((n_tiles_max * tm,), jnp.int32).at[dest].set(token, unique_indices=True)
    return dest, tile_expert, n_tiles, row_token


def _gather_rows_body(idx_ref, src_ref, o_ref, sem, *, tile):
    base = pl.program_id(0) * tile

    def start(r, carry):
        pltpu.make_async_copy(src_ref.at[idx_ref[base + r]], o_ref.at[r], sem).start()
        return carry

    lax.fori_loop(0, tile, start, 0, unroll=8)
    pltpu.make_async_copy(src_ref.at[pl.ds(0, tile)], o_ref, sem).wait()


def gather_rows(src3, idx, tile=256):
    n = idx.shape[0]
    slab = src3.shape[1:]
    return pl.pallas_call(
        functools.partial(_gather_rows_body, tile=tile),
        grid_spec=pltpu.PrefetchScalarGridSpec(
            num_scalar_prefetch=1, grid=(n // tile,),
            in_specs=[pl.BlockSpec(memory_space=pl.ANY)],
            out_specs=pl.BlockSpec((tile,) + slab, lambda i, idx_ref: (i, 0, 0)),
            scratch_shapes=[pltpu.SemaphoreType.DMA(())]),
        out_shape=jax.ShapeDtypeStruct((n,) + slab, src3.dtype),
        compiler_params=_cparams(("arbitrary",)),
    )(idx, src3)


def _live_tile(t, nt_ref):
    return jnp.minimum(t, nt_ref[0] - 1)


def _moe_gu_body(te_ref, nt_ref, a_ref, bg_ref, bu_ref, o_ref):
    t = pl.program_id(1)

    @pl.when(t < nt_ref[0])
    def _():
        a = a_ref[...]
        g = _dot(a, bg_ref[...])
        u = _dot(a, bu_ref[...])
        o_ref[...] = (g * jax.nn.sigmoid(g) * u).astype(o_ref.dtype)

    @pl.when(t >= nt_ref[0])
    def _():
        o_ref[...] = jnp.zeros_like(o_ref)


def moe_gate_up(xs, w_gu, tile_expert, n_tiles, tm, tn=512):
    rows, k = xs.shape
    f = w_gu.shape[2] // 2
    nj = f // tn
    return pl.pallas_call(
        _moe_gu_body,
        grid_spec=pltpu.PrefetchScalarGridSpec(
            num_scalar_prefetch=2, grid=(nj, rows // tm),
            in_specs=[pl.BlockSpec((tm, k), lambda j, t, te, nt: (_live_tile(t, nt), 0)),
                      pl.BlockSpec((None, k, tn), lambda j, t, te, nt: (te[_live_tile(t, nt)], 0, j)),
                      pl.BlockSpec((None, k, tn), lambda j, t, te, nt: (te[_live_tile(t, nt)], 0, nj + j))],
            out_specs=pl.BlockSpec((tm, tn), lambda j, t, te, nt: (t, j))),
        out_shape=jax.ShapeDtypeStruct((rows, f), BF16),
        compiler_params=_cparams(("arbitrary", "arbitrary")),
    )(tile_expert, n_tiles, xs, w_gu, w_gu)


def _moe_down_body(te_ref, nt_ref, a_ref, b_ref, o_ref):
    t = pl.program_id(1)

    @pl.when(t < nt_ref[0])
    def _():
        o_ref[...] = _dot(a_ref[...], b_ref[...]).astype(o_ref.dtype)

    @pl.when(t >= nt_ref[0])
    def _():
        o_ref[...] = jnp.zeros_like(o_ref)


def moe_down(act, w_down, tile_expert, n_tiles, tm, tn=1024):
    rows, f = act.shape
    n = w_down.shape[2]
    return pl.pallas_call(
        _moe_down_body,
        grid_spec=pltpu.PrefetchScalarGridSpec(
            num_scalar_prefetch=2, grid=(n // tn, rows // tm),
            in_specs=[pl.BlockSpec((tm, f), lambda j, t, te, nt: (_live_tile(t, nt), 0)),
                      pl.BlockSpec((None, f, tn), lambda j, t, te, nt: (te[_live_tile(t, nt)], 0, j))],
            out_specs=pl.BlockSpec((tm, tn), lambda j, t, te, nt: (t, j))),
        out_shape=jax.ShapeDtypeStruct((rows, n), BF16),
        compiler_params=_cparams(("arbitrary", "arbitrary")),
    )(tile_expert, n_tiles, act, w_down)


def _combine_body(x_ref, y1_ref, y2_ref, w_ref, o_ref):
    w = w_ref[...]
    o_ref[...] = (x_ref[...] + _lane_col(w, 0) * y1_ref[...].astype(F32)
                  + _lane_col(w, 1) * y2_ref[...].astype(F32))


def moe_combine(x, yg, wts, tm=256):
    m, d = x.shape
    nb = m // tm
    return pl.pallas_call(
        _combine_body,
        grid=(nb,),
        in_specs=[pl.BlockSpec((tm, d), lambda i: (i, 0)), pl.BlockSpec((tm, d), lambda i: (i, 0)),
                  pl.BlockSpec((tm, d), lambda i: (nb + i, 0)), pl.BlockSpec((tm, LANES), lambda i: (i, 0))],
        out_specs=pl.BlockSpec((tm, d), lambda i: (i, 0)),
        out_shape=jax.ShapeDtypeStruct((m, d), F32),
        compiler_params=_cparams(("parallel",)),
    )(x, yg, yg, wts)


def moe_ffn(x, g, router, w_gu, w_down, tm=MOE_TM):
    m, d = x.shape
    h, idx, wts = norm_and_route(x, g, router)
    dest, tile_expert, n_tiles, row_token = _route_plan(idx, tm)
    rows = row_token.shape[0]
    xs3 = gather_rows(h.reshape(m, ROW_SLABS, d // ROW_SLABS), row_token)
    act = moe_gate_up(xs3.reshape(rows, d), w_gu, tile_expert, n_tiles, tm)
    ys = moe_down(act, w_down, tile_expert, n_tiles, tm)
    yg3 = gather_rows(ys.reshape(rows, ROW_SLABS, d // ROW_SLABS), dest)
    return moe_combine(x, yg3.reshape(2 * m, d), wts)


def _split_w_in(w_in):
    a_w = BRANCH_W
    o_ml = 3 * a_w
    o_if = o_ml + 4 * a_w
    o_sg = o_if + 2 * N_HEADS
    o_swq = o_sg + 2 * a_w
    o_swk = o_swq + SW_QHEADS * SW_DIM
    o_gate = o_swk + 2 * SW_KVHEADS * SW_DIM
    main = jnp.concatenate([w_in[:, :o_if], w_in[:, o_sg:o_swk], w_in[:, o_gate:]], axis=1).astype(BF16)
    pad = jnp.zeros((w_in.shape[0], LANES - 2 * N_HEADS), w_in.dtype)
    tail = jnp.concatenate([w_in[:, o_swk:o_gate], w_in[:, o_if:o_sg], pad], axis=1).astype(BF16)
    return main, tail


def _mixer(h, layer_params, batch, seq):
    (w_in, conv_w, conv_b, gate_b, head_norm, sg_w, sg_b, q_norm, k_norm, sinks, w_branch, w_out) = layer_params
    w_main, w_tail = _split_w_in(w_in)
    proj = inproj_main(h, w_main)
    tail = inproj_tail(h, w_tail)
    y_a = sb_attention(proj, batch, seq)
    y_b = mlstm(proj, tail, conv_w, conv_b, gate_b, head_norm, batch, seq)
    y_c = spatial_gating(proj, sg_w, sg_b)
    y_d = swa_attention(proj, tail, q_norm, k_norm, sinks, seq)
    merged = gated_merge((y_a, y_b, y_c, y_d), w_branch.astype(BF16), proj)
    return merged, w_out.astype(BF16)


def kernel(x, norm_mix, w_in, ml_conv_w, ml_conv_b, ml_gate_b, ml_head_norm, sg_w, sg_b, sw_q_norm, sw_k_norm,
           sw_sinks, w_branch, w_out, norm_ffn, ffn_w_gu, ffn_w_down, moe_router, moe_w_gu, moe_w_down):
    batch, seq, d = x.shape
    depth = norm_mix.shape[0]
    xf = x.reshape(batch * seq, d)
    for layer in range(depth):
        h = rmsnorm(xf, norm_mix[layer])
        merged, w_o = _mixer(h, (w_in[layer], ml_conv_w[layer], ml_conv_b[layer], ml_gate_b[layer],
                                 ml_head_norm[layer], sg_w[layer], sg_b[layer], sw_q_norm[layer],
                                 sw_k_norm[layer], sw_sinks[layer], w_branch[layer], w_out[layer]),
                             batch, seq)
        xf = matmul_residual(merged, w_o, xf)
        j = layer // 2
        if layer % 2 == 0:
            h = rmsnorm(xf, norm_ffn[layer])
            act = ffn_gate_up(h, ffn_w_gu[j].astype(BF16))
            xf = ffn_down(act, ffn_w_down[j].astype(BF16), xf)
        else:
            xf = moe_ffn(xf, norm_ffn[layer], moe_router[j], moe_w_gu[j].astype(BF16), moe_w_down[j].astype(BF16))
    return xf.reshape(batch, seq, d)
```

```python
import functools
import math
from typing import NamedTuple

import jax
import jax.numpy as jnp
from jax import lax
from jax.experimental import pallas as pl
from jax.experimental.pallas import tpu as pltpu

F32 = jnp.float32
BF16 = jnp.bfloat16
EPS = 1e-6
LANES = 128
CHUNK = 128
VMEM_LIMIT_MB = 56

D_MODEL = 4096
N_HEADS = 8
HEAD_DIM = 128
BRANCH_W = N_HEADS * HEAD_DIM
SW_QHEADS, SW_KVHEADS, SW_DIM = 16, 2, 64
N_BRANCH = 4
N_EXPERTS = 8

CB_SB_Q, CB_SB_K, CB_SB_V = 0, 1, 2
CB_ML_Q, CB_ML_K, CB_ML_V, CB_ML_O = 3, 4, 5, 6
CB_SG_U, CB_SG_V = 7, 8
CB_SW_Q = 9
CB_GATE0 = 10
N_MAIN = (CB_GATE0 + N_BRANCH * D_MODEL // BRANCH_W) * BRANCH_W
N_TAIL = 3 * LANES


def _cparams(sem, vmem_mb=VMEM_LIMIT_MB):
    return pltpu.CompilerParams(dimension_semantics=sem, vmem_limit_bytes=vmem_mb * 1024 * 1024)


def _dot(a, b):
    return jnp.dot(a, b, preferred_element_type=F32)


def _dot_nt(a, b):
    return lax.dot_general(a, b, (((1,), (1,)), ((), ())), preferred_element_type=F32)


def _dot_tn(a, b):
    return lax.dot_general(a, b, (((0,), (0,)), ((), ())), preferred_element_type=F32)


def _dot_f32(a, b):
    return jnp.dot(a, b, preferred_element_type=F32, precision=lax.Precision.HIGHEST)


def _lane_col(x, idx):
    lane = lax.broadcasted_iota(jnp.int32, x.shape, 1)
    return jnp.sum(jnp.where(lane == idx, x, 0.0), axis=1, keepdims=True)


def _softplus(z):
    return jnp.maximum(z, 0.0) + jnp.log(1.0 + jnp.exp(-jnp.abs(z)))


class SideCast(NamedTuple):
    src: jax.Array
    block_rows: int
    outs: tuple


def _plain_cast(src2d, block_rows):
    cols = src2d.shape[1]
    return SideCast(src2d, block_rows, ((cols, ((0, cols),)),))


def _hosted_call(body, grid, in_specs, out_specs, out_shape, args, sides):
    n_in, n_out, n_side = len(in_specs), len(out_specs), len(sides)
    steps = math.prod(grid)

    def linear(*ids):
        s = ids[0]
        for g, i in zip(grid[1:], ids[1:]):
            s = s * g + i
        return s

    in_specs, out_specs, out_shape, args = list(in_specs), list(out_specs), list(out_shape), list(args)
    for sc in sides:
        rows, cols = sc.src.shape
        br = sc.block_rows
        while rows % br or rows // br > steps:
            br += 16
        nblk = rows // br

        def imap(*ids, nblk=nblk):
            return (jnp.minimum(linear(*ids), nblk - 1), 0)

        in_specs.append(pl.BlockSpec((br, cols), imap))
        args.append(sc.src)
        for width, _ in sc.outs:
            out_specs.append(pl.BlockSpec((br, width), imap))
            out_shape.append(jax.ShapeDtypeStruct((rows, width), BF16))

    def hosted(*refs):
        outs = refs[n_in + n_side:]
        body(*refs[:n_in], *outs[:n_out])
        k = n_out
        for s_ref, sc in zip(refs[n_in:n_in + n_side], sides):
            v = s_ref[...]
            for width, pieces in sc.outs:
                dst = outs[k]
                k += 1
                off = 0
                for lo, hi in pieces:
                    dst[:, off:off + hi - lo] = v[:, lo:hi].astype(BF16)
                    off += hi - lo
                if off < width:
                    dst[:, off:width] = jnp.zeros((v.shape[0], width - off), BF16)

    res = pl.pallas_call(
        hosted, grid=grid, in_specs=in_specs, out_specs=out_specs, out_shape=out_shape,
        compiler_params=_cparams(("arbitrary",) * len(grid)),
    )(*args)
    return res[:n_out], res[n_out:]


def _rmsnorm_body(x_ref, g_ref, o_ref):
    x = x_ref[...]
    ms = jnp.mean(x * x, axis=-1, keepdims=True)
    o_ref[...] = (x * lax.rsqrt(ms + EPS) * g_ref[...]).astype(o_ref.dtype)


def rmsnorm(x, g, tm=256):
    m, d = x.shape
    return pl.pallas_call(
        _rmsnorm_body,
        grid=(m // tm,),
        in_specs=[pl.BlockSpec((tm, d), lambda i: (i, 0)), pl.BlockSpec((1, d), lambda i: (0, 0))],
        out_specs=pl.BlockSpec((tm, d), lambda i: (i, 0)),
        out_shape=jax.ShapeDtypeStruct((m, d), BF16),
        compiler_params=_cparams(("parallel",)),
    )(x, g.reshape(1, d))


def _mm_body(a_ref, b_ref, o_ref):
    o_ref[...] = _dot(a_ref[...], b_ref[...]).astype(o_ref.dtype)


def inproj_main(h, w, tm=1024, tn=1024, sides=()):
    m, k = h.shape
    n = w.shape[1]
    (proj,), side_outs = _hosted_call(
        _mm_body, (n // tn, m // tm),
        [pl.BlockSpec((tm, k), lambda j, i: (i, 0)), pl.BlockSpec((k, tn), lambda j, i: (0, j))],
        [pl.BlockSpec((tm, tn), lambda j, i: (i, j))],
        [jax.ShapeDtypeStruct((m, n), BF16)], (h, w), sides)
    return proj, side_outs


def inproj_tail(h, w, tm=1024):
    m, k = h.shape
    n = w.shape[1]
    return pl.pallas_call(
        _mm_body,
        grid=(m // tm,),
        in_specs=[pl.BlockSpec((tm, k), lambda i: (i, 0)), pl.BlockSpec((k, n), lambda i: (0, 0))],
        out_specs=pl.BlockSpec((tm, n), lambda i: (i, 0)),
        out_shape=jax.ShapeDtypeStruct((m, n), F32),
        compiler_params=_cparams(("parallel",)),
    )(h, w)


def _mm_res_body(a_ref, b_ref, x_ref, o_ref):
    o_ref[...] = x_ref[...] + _dot(a_ref[...], b_ref[...])


def matmul_residual(a, w, x, tm=512, tn=1024):
    m, k = a.shape
    n = w.shape[1]
    return pl.pallas_call(
        _mm_res_body,
        grid=(n // tn, m // tm),
        in_specs=[pl.BlockSpec((tm, k), lambda j, i: (i, 0)), pl.BlockSpec((k, tn), lambda j, i: (0, j)),
                  pl.BlockSpec((tm, tn), lambda j, i: (i, j))],
        out_specs=pl.BlockSpec((tm, tn), lambda j, i: (i, j)),
        out_shape=jax.ShapeDtypeStruct((m, n), F32),
        compiler_params=_cparams(("parallel", "parallel")),
    )(a, w, x)


def _merge_body(ya_ref, yb_ref, yc_ref, yd_ref, wa_ref, wb_ref, wc_ref, wd_ref,
                ga_ref, gb_ref, gc_ref, gd_ref, o_ref):
    def gate(g_ref):
        return jax.nn.sigmoid(g_ref[...].astype(F32))

    acc = gate(ga_ref) * _dot(ya_ref[...], wa_ref[...])
    acc = acc + gate(gb_ref) * _dot(yb_ref[...], wb_ref[...])
    acc = acc + gate(gc_ref) * _dot(yc_ref[...], wc_ref[...])
    acc = acc + gate(gd_ref) * _dot(yd_ref[...], wd_ref[...])
    o_ref[...] = acc.astype(o_ref.dtype)


def gated_merge(ys, w_branch, proj, tm=512, tn=1024):
    m, kb = ys[0].shape
    d = w_branch.shape[2]
    gpb = d // tn
    g0 = CB_GATE0 * BRANCH_W // tn
    y_specs = [pl.BlockSpec((tm, kb), lambda j, i: (i, 0)) for _ in range(N_BRANCH)]
    w_specs = [pl.BlockSpec((None, kb, tn), functools.partial(lambda j, i, br: (br, 0, j), br=br))
               for br in range(N_BRANCH)]
    g_specs = [pl.BlockSpec((tm, tn), functools.partial(lambda j, i, br: (i, g0 + br * gpb + j), br=br))
               for br in range(N_BRANCH)]
    return pl.pallas_call(
        _merge_body,
        grid=(d // tn, m // tm),
        in_specs=y_specs + w_specs + g_specs,
        out_specs=pl.BlockSpec((tm, tn), lambda j, i: (i, j)),
        out_shape=jax.ShapeDtypeStruct((m, d), BF16),
        compiler_params=_cparams(("parallel", "parallel")),
    )(*ys, w_branch, w_branch, w_branch, w_branch, proj, proj, proj, proj)


def _sb_body(q_ref, k_ref, v_ref, o_ref, *, tq, tk, scale):
    qi = pl.program_id(2)
    r = tq // tk
    q = q_ref[...]
    ur = lax.broadcasted_iota(jnp.int32, (tk, 2 * tk), 0)
    uc = lax.broadcasted_iota(jnp.int32, (tk, 2 * tk), 1)
    cum_w = jnp.where((uc >= tk) | (ur > uc), 1.0, 0.0).astype(BF16)

    def key_block(start, acc, c, mask):
        k = k_ref[pl.ds(start, tq), :]
        v = v_ref[pl.ds(start, tq), :]
        z = _dot_nt(q, k) * scale
        nz = -z
        lg = jnp.log(1.0 + jnp.exp(jnp.minimum(z, nz)))
        l1mb = jnp.minimum(nz, 0.0) - lg
        log_beta = l1mb + z
        if mask is not None:
            l1mb = jnp.where(mask, l1mb, 0.0)
        l1mb_b = l1mb.astype(BF16)
        ws = [None] * r
        for j in range(r - 1, -1, -1):
            sl = slice(j * tk, (j + 1) * tk)
            t = _dot(l1mb_b[:, sl], cum_w)
            w = jnp.exp(log_beta[:, sl] + t[:, :tk] + c)
            if mask is not None:
                w = jnp.where(mask[:, sl], w, 0.0)
            ws[j] = w.astype(BF16)
            c = c + t[:, tk:]
        acc = acc + _dot(jnp.concatenate(ws, axis=1), v)
        return acc, c

    acc = jnp.zeros((tq, HEAD_DIM), F32)
    c = jnp.zeros((tq, tk), F32)
    rows = lax.broadcasted_iota(jnp.int32, (tq, tq), 0)
    cols = lax.broadcasted_iota(jnp.int32, (tq, tq), 1)
    acc, c = key_block(pl.multiple_of(qi * tq, tq), acc, c, cols < rows)

    def body(it, carry):
        return key_block(pl.multiple_of((qi - 1 - it) * tq, tq), *carry, None)

    acc, c = lax.fori_loop(0, qi, body, (acc, c))
    o_ref[...] = acc.astype(o_ref.dtype)


def sb_attention(proj, batch, seq, tq=512, tk=128):
    m = proj.shape[0]
    nq = seq // tq
    return pl.pallas_call(
        functools.partial(_sb_body, tq=tq, tk=tk, scale=HEAD_DIM ** -0.5),
        grid=(batch, N_HEADS, nq),
        in_specs=[pl.BlockSpec((tq, HEAD_DIM), lambda b, h, i: (b * nq + i, CB_SB_Q * N_HEADS + h)),
                  pl.BlockSpec((seq, HEAD_DIM), lambda b, h, i: (b, CB_SB_K * N_HEADS + h)),
                  pl.BlockSpec((seq, HEAD_DIM), lambda b, h, i: (b, CB_SB_V * N_HEADS + h))],
        out_specs=pl.BlockSpec((tq, HEAD_DIM), lambda b, h, i: (b * nq + i, h)),
        out_shape=jax.ShapeDtypeStruct((m, BRANCH_W), BF16),
        compiler_params=_cparams(("parallel", "parallel", "arbitrary")),
    )(proj, proj, proj)


def _mlstm_body(q_ref, k_ref, v_ref, o_ref, g_ref, cw_ref, cb_ref, gb_ref, hn_ref, y_ref,
                xs_ref, c_ref, n_ref, m_ref, *, n_conv):
    L = CHUNK
    chunk = pl.program_id(1)

    @pl.when(chunk == 0)
    def _():
        xs_ref[0:8, :] = jnp.zeros((8, 2 * BRANCH_W), F32)
        c_ref[...] = jnp.zeros_like(c_ref)
        n_ref[...] = jnp.zeros_like(n_ref)
        m_ref[...] = jnp.zeros_like(m_ref)

    xs_ref[8:8 + L, 0:BRANCH_W] = q_ref[...].astype(F32)
    xs_ref[8:8 + L, BRANCH_W:2 * BRANCH_W] = k_ref[...].astype(F32)
    conv = jnp.zeros((L, 2 * BRANCH_W), F32) + cb_ref[...]
    for j in range(n_conv):
        conv = conv + cw_ref[j:j + 1, :] * xs_ref[pl.ds(8 - (n_conv - 1) + j, L), :]
    xs_ref[0:8, :] = xs_ref[L:L + 8, :]
    qk = conv * jax.nn.sigmoid(conv)
    qf = qk[:, 0:BRANCH_W]
    kf = qk[:, BRANCH_W:2 * BRANCH_W] * (HEAD_DIM ** -0.5)

    g = g_ref[...] + gb_ref[...]
    lf = -_softplus(-g)
    ri = lax.broadcasted_iota(jnp.int32, (L, L), 0)
    ci = lax.broadcasted_iota(jnp.int32, (L, L), 1)
    tril = ri >= ci
    tril_f = jnp.where(tril, 1.0, 0.0).astype(F32)
    triu_f = jnp.where(ri <= ci, 1.0, 0.0).astype(F32)
    b_cols = _dot_f32(tril_f, lf)
    g_t = g.T
    b_rows = _dot_f32(lf.T, triu_f)

    for h in range(N_HEADS):
        sl = slice(h * HEAD_DIM, (h + 1) * HEAD_DIM)
        qh = qf[:, sl]
        kh = kf[:, sl]
        qb = qh.astype(BF16)
        kb = kh.astype(BF16)
        vb = v_ref[:, sl]
        i_col = _lane_col(g, h)
        b_col = _lane_col(b_cols, N_HEADS + h)
        i_row = g_t[h:h + 1, :]
        b_row = b_rows[N_HEADS + h:N_HEADS + h + 1, :]
        m_prev = m_ref[h:h + 1, 0:1]
        c_prev = c_ref[h]
        n_prev = n_ref[h:h + 1, :]

        log_d = jnp.where(tril, b_col - b_row + i_row, -jnp.inf)
        m_inter = b_col + m_prev
        m_t = jnp.maximum(m_inter, jnp.max(log_d, axis=1, keepdims=True))
        dmat = jnp.exp(log_d - m_t)
        s = _dot_nt(qb, kb) * dmat
        inter = jnp.exp(m_inter - m_t)
        num = _dot(s.astype(BF16), vb) + inter * _dot(qb, c_prev.astype(BF16))
        den = jnp.sum(s, axis=1, keepdims=True) + inter * jnp.sum(qb.astype(F32) * n_prev, axis=1, keepdims=True)
        hh = num / jnp.maximum(jnp.abs(den), jnp.exp(-m_t))

        m_new = m_t[L - 1:L, :]
        b_last = b_col[L - 1:L, :]
        w_col = jnp.exp(b_last - b_col + i_col - m_new)
        decay = jnp.exp(b_last + m_prev - m_new)
        kw = kh * w_col
        c_ref[h] = decay * c_prev + _dot_tn(kw.astype(BF16), vb)
        n_ref[h:h + 1, :] = decay * n_prev + jnp.sum(kw, axis=0, keepdims=True)
        m_ref[h:h + 1, :] = jnp.broadcast_to(m_new, (1, LANES))

        hn = hh * lax.rsqrt(jnp.mean(hh * hh, axis=1, keepdims=True) + EPS) * hn_ref[:, sl]
        y_ref[:, sl] = (jax.nn.sigmoid(o_ref[:, sl].astype(F32)) * hn).astype(y_ref.dtype)


def mlstm(proj, tail, conv_w, conv_b, gate_b, head_norm, batch, seq):
    m = proj.shape[0]
    nc = seq // CHUNK
    n_conv = conv_w.shape[0]
    gb = jnp.zeros((1, LANES), F32).at[0, 0:2 * N_HEADS].set(gate_b.reshape(-1))

    def blk(cb):
        return pl.BlockSpec((CHUNK, BRANCH_W), lambda b, c: (b * nc + c, cb))

    def const(shape):
        return pl.BlockSpec(shape, lambda b, c: (0, 0))

    return pl.pallas_call(
        functools.partial(_mlstm_body, n_conv=n_conv),
        grid=(batch, nc),
        in_specs=[blk(CB_ML_Q), blk(CB_ML_K), blk(CB_ML_V), blk(CB_ML_O),
                  pl.BlockSpec((CHUNK, LANES), lambda b, c: (b * nc + c, 2)),
                  const((n_conv, 2 * BRANCH_W)), const((1, 2 * BRANCH_W)), const((1, LANES)),
                  const((1, BRANCH_W))],
        out_specs=pl.BlockSpec((CHUNK, BRANCH_W), lambda b, c: (b * nc + c, 0)),
        out_shape=jax.ShapeDtypeStruct((m, BRANCH_W), BF16),
        scratch_shapes=[pltpu.VMEM((CHUNK + 8, 2 * BRANCH_W), F32),
                        pltpu.VMEM((N_HEADS, HEAD_DIM, HEAD_DIM), F32),
                        pltpu.VMEM((N_HEADS, HEAD_DIM), F32),
                        pltpu.VMEM((N_HEADS, LANES), F32)],
        compiler_params=_cparams(("parallel", "arbitrary")),
    )(proj, proj, proj, proj, tail, conv_w, conv_b.reshape(1, -1), gb, head_norm.reshape(1, -1))


def _gelu(x):
    return 0.5 * x * (1.0 + lax.erf(x * (2.0 ** -0.5)))


def _sgu_body(u_ref, v_ref, w_ref, b_ref, o_ref):
    L = CHUNK
    u = _gelu(u_ref[...].astype(F32))
    v = _gelu(v_ref[...].astype(F32))
    mu = jnp.mean(v, axis=1, keepdims=True)
    vc = v - mu
    var = jnp.mean(vc * vc, axis=1, keepdims=True)
    vn = (vc * lax.rsqrt(var + EPS)).astype(BF16)
    ri = lax.broadcasted_iota(jnp.int32, (L, L), 0)
    ci = lax.broadcasted_iota(jnp.int32, (L, L), 1)
    tril = ri >= ci
    bias = b_ref[...]
    for g in range(N_HEADS):
        sl = slice(g * HEAD_DIM, (g + 1) * HEAD_DIM)
        wg = jnp.where(tril, w_ref[g], 0.0).astype(BF16)
        mixed = _dot(wg, vn[:, sl]) + _lane_col(bias, g)
        o_ref[:, sl] = (u[:, sl] * mixed).astype(o_ref.dtype)


def spatial_gating(proj, sg_w, sg_b):
    m = proj.shape[0]
    bias = jnp.zeros((CHUNK, LANES), F32).at[:, 0:N_HEADS].set(sg_b.T)
    return pl.pallas_call(
        _sgu_body,
        grid=(m // CHUNK,),
        in_specs=[pl.BlockSpec((CHUNK, BRANCH_W), lambda i: (i, CB_SG_U)),
                  pl.BlockSpec((CHUNK, BRANCH_W), lambda i: (i, CB_SG_V)),
                  pl.BlockSpec((N_HEADS, CHUNK, CHUNK), lambda i: (0, 0, 0)),
                  pl.BlockSpec((CHUNK, LANES), lambda i: (0, 0))],
        out_specs=pl.BlockSpec((CHUNK, BRANCH_W), lambda i: (i, 0)),
        out_shape=jax.ShapeDtypeStruct((m, BRANCH_W), BF16),
        compiler_params=_cparams(("parallel",)),
    )(proj, proj, sg_w, bias)


def _swa_body(q_ref, kp_ref, kc_ref, vp_ref, vc_ref, qg_ref, kg_ref, sink_ref, o_ref, *, n_chunks):
    W = CHUNK
    n = pl.program_id(0) % n_chunks
    lane = lax.broadcasted_iota(jnp.int32, (1, LANES), 1)
    lo = lane < SW_DIM

    def pair_norm(x, gain):
        x2 = x * x
        s_lo = jnp.sum(jnp.where(lo, x2, 0.0), axis=1, keepdims=True)
        s_hi = jnp.sum(jnp.where(lo, 0.0, x2), axis=1, keepdims=True)
        ms = jnp.where(lo, s_lo, s_hi) * (1.0 / SW_DIM)
        return x * lax.rsqrt(ms + EPS) * gain

    kg = kg_ref[...]
    kk = jnp.concatenate([pair_norm(kp_ref[...], kg), pair_norm(kc_ref[...], kg)], axis=0)
    vv = jnp.concatenate([vp_ref[...], vc_ref[...]], axis=0)
    kk_sw = pltpu.roll(kk, SW_DIM, axis=1)
    vv_sw = pltpu.roll(vv, SW_DIM, axis=1)
    k_ver = [[jnp.where(lo, kk, 0.0).astype(BF16), jnp.where(lo, 0.0, kk_sw).astype(BF16)],
             [jnp.where(lo, kk_sw, 0.0).astype(BF16), jnp.where(lo, 0.0, kk).astype(BF16)]]
    v_ver = [[jnp.where(lo, vv, 0.0).astype(BF16), jnp.where(lo, 0.0, vv_sw).astype(BF16)],
             [jnp.where(lo, vv_sw, 0.0).astype(BF16), jnp.where(lo, 0.0, vv).astype(BF16)]]

    stack = SW_QHEADS // SW_KVHEADS // 2
    qi = lax.broadcasted_iota(jnp.int32, (stack * W, 2 * W), 0) % W
    kj = lax.broadcasted_iota(jnp.int32, (stack * W, 2 * W), 1)
    diff = W + qi - kj
    valid = (diff >= 0) & (diff < W) & ((n > 0) | (kj >= W))
    scale = SW_DIM ** -0.5
    qg = qg_ref[...]
    qn = [pair_norm(q_ref[:, pb * LANES:(pb + 1) * LANES].astype(F32), qg) for pb in range(SW_QHEADS // 2)]
    for hk in range(SW_KVHEADS):
        outs = []
        for half in range(2):
            sel = lo if half == 0 else jnp.logical_not(lo)
            qs = jnp.concatenate([jnp.where(sel, qn[hk * stack + i], 0.0) for i in range(stack)], axis=0)
            s = jnp.where(valid, _dot_nt(qs.astype(BF16), k_ver[hk][half]) * scale, -jnp.inf)
            st = hk * 2 + half
            sink = sink_ref[st * stack * W:(st + 1) * stack * W, :]
            mx = jnp.maximum(jnp.max(s, axis=1, keepdims=True), sink)
            p = jnp.exp(s - mx)
            inv = 1.0 / (jnp.sum(p, axis=1, keepdims=True) + jnp.exp(sink - mx))
            outs.append(_dot(p.astype(BF16), v_ver[hk][half]) * inv)
        for i in range(stack):
            pb = hk * stack + i
            o_ref[:, pb * LANES:(pb + 1) * LANES] = (outs[0][i * W:(i + 1) * W]
                                                     + outs[1][i * W:(i + 1) * W]).astype(o_ref.dtype)


def swa_attention(proj, tail, q_norm, k_norm, sinks, seq):
    m = proj.shape[0]
    nc = seq // CHUNK
    qg = jnp.tile(q_norm.reshape(1, SW_DIM), (1, LANES // SW_DIM))
    kg = jnp.tile(k_norm.reshape(1, SW_DIM), (1, LANES // SW_DIM))
    stack = SW_QHEADS // SW_KVHEADS // 2
    order = [2 * (hk * stack + i) + half for hk in range(SW_KVHEADS) for half in range(2) for i in range(stack)]
    sink_rows = jnp.repeat(sinks.astype(F32)[jnp.array(order)], CHUNK).reshape(SW_QHEADS * CHUNK, 1)

    def prev(i):
        return jnp.maximum(i - 1, 0)

    return pl.pallas_call(
        functools.partial(_swa_body, n_chunks=nc),
        grid=(m // CHUNK,),
        in_specs=[pl.BlockSpec((CHUNK, BRANCH_W), lambda i: (i, CB_SW_Q)),
                  pl.BlockSpec((CHUNK, LANES), lambda i: (prev(i), 0)),
                  pl.BlockSpec((CHUNK, LANES), lambda i: (i, 0)),
                  pl.BlockSpec((CHUNK, LANES), lambda i: (prev(i), 1)),
                  pl.BlockSpec((CHUNK, LANES), lambda i: (i, 1)),
                  pl.BlockSpec((1, LANES), lambda i: (0, 0)),
                  pl.BlockSpec((1, LANES), lambda i: (0, 0)),
                  pl.BlockSpec((SW_QHEADS * CHUNK, 1), lambda i: (0, 0))],
        out_specs=pl.BlockSpec((CHUNK, BRANCH_W), lambda i: (i, 0)),
        out_shape=jax.ShapeDtypeStruct((m, BRANCH_W), BF16),
        compiler_params=_cparams(("parallel",)),
    )(proj, tail, tail, tail, tail, qg, kg, sink_rows)


def _ffn_gu_body(a_ref, bg_ref, bu_ref, o_ref):
    a = a_ref[...]
    g = _dot(a, bg_ref[...])
    u = _dot(a, bu_ref[...])
    o_ref[...] = (g * jax.nn.sigmoid(g) * u).astype(o_ref.dtype)


def ffn_gate_up(h, w_gu, tm=1024, tn=512, sides=()):
    m, k = h.shape
    f = w_gu.shape[1] // 2
    nj = f // tn
    (act,), side_outs = _hosted_call(
        _ffn_gu_body, (nj, m // tm),
        [pl.BlockSpec((tm, k), lambda j, i: (i, 0)),
         pl.BlockSpec((k, tn), lambda j, i: (0, j)),
         pl.BlockSpec((k, tn), lambda j, i: (0, nj + j))],
        [pl.BlockSpec((tm, tn), lambda j, i: (i, j))],
        [jax.ShapeDtypeStruct((m, f), BF16)], (h, w_gu, w_gu), sides)
    return act, side_outs


def _ffn_down_body(a_ref, b_ref, x_ref, o_ref):
    kk = pl.program_id(2)
    part = _dot(a_ref[...], b_ref[...])

    @pl.when(kk == 0)
    def _():
        o_ref[...] = x_ref[...] + part

    @pl.when(kk > 0)
    def _():
        o_ref[...] += part


def ffn_down(act, w_down, x, tm=1024, tn=1024, tk=3584):
    m, f = act.shape
    n = w_down.shape[1]
    nk = f // tk
    return pl.pallas_call(
        _ffn_down_body,
        grid=(n // tn, m // tm, nk),
        in_specs=[pl.BlockSpec((tm, tk), lambda j, i, k: (i, k)),
                  pl.BlockSpec((tk, tn), lambda j, i, k: (k, j)),
                  pl.BlockSpec((tm, tn), lambda j, i, k: (i, j))],
        out_specs=pl.BlockSpec((tm, tn), lambda j, i, k: (i, j)),
        out_shape=jax.ShapeDtypeStruct((m, n), F32),
        compiler_params=_cparams(("parallel", "parallel", "arbitrary")),
    )(act, w_down, x)


MOE_TM = 512
ROW_SLABS = 32


def _router_body(x_ref, g_ref, r_ref, h_ref, idx_ref, wts_ref):
    x = x_ref[...]
    ms = jnp.mean(x * x, axis=-1, keepdims=True)
    h = x * lax.rsqrt(ms + EPS) * g_ref[...]
    h_ref[...] = h.astype(h_ref.dtype)
    logits = _dot_f32(h, r_ref[...])
    lane = lax.broadcasted_iota(jnp.int32, logits.shape, 1)
    logits = jnp.where(lane < N_EXPERTS, logits, -jnp.inf)
    m1 = jnp.max(logits, axis=1, keepdims=True)
    i1 = jnp.min(jnp.where(logits == m1, lane, LANES), axis=1, keepdims=True)
    rest = jnp.where(lane == i1, -jnp.inf, logits)
    m2 = jnp.max(rest, axis=1, keepdims=True)
    i2 = jnp.min(jnp.where(rest == m2, lane, LANES), axis=1, keepdims=True)
    e2 = jnp.exp(m2 - m1)
    w1 = 1.0 / (1.0 + e2)
    w2 = e2 / (1.0 + e2)
    idx_ref[...] = jnp.where(lane == 0, i1, jnp.where(lane == 1, i2, 0))
    wts_ref[...] = jnp.where(lane == 0, w1, jnp.where(lane == 1, w2, 0.0))


def norm_and_route(x, g, router, tm=256):
    m, d = x.shape
    r_pad = jnp.zeros((d, LANES), F32).at[:, 0:N_EXPERTS].set(router)
    return pl.pallas_call(
        _router_body,
        grid=(m // tm,),
        in_specs=[pl.BlockSpec((tm, d), lambda i: (i, 0)), pl.BlockSpec((1, d), lambda i: (0, 0)),
                  pl.BlockSpec((d, LANES), lambda i: (0, 0))],
        out_specs=[pl.BlockSpec((tm, d), lambda i: (i, 0)), pl.BlockSpec((tm, LANES), lambda i: (i, 0)),
                   pl.BlockSpec((tm, LANES), lambda i: (i, 0))],
        out_shape=[jax.ShapeDtypeStruct((m, d), BF16), jax.ShapeDtypeStruct((m, LANES), jnp.int32),
                   jax.ShapeDtypeStruct((m, LANES), F32)],
        compiler_params=_cparams(("parallel",)),
    )(x, g.reshape(1, d), r_pad)


def _route_plan(idx, tm):
    m = idx.shape[0]
    n_tiles_max = -(-(2 * m + N_EXPERTS * (tm - 1)) // tm)
    e_flat = jnp.concatenate([idx[:, 0], idx[:, 1]])
    onehot = (e_flat[:, None] == jnp.arange(N_EXPERTS, dtype=jnp.int32)[None, :]).astype(jnp.int32)
    csum = jnp.cumsum(onehot, axis=0)
    counts = csum[-1]
    rank = jnp.sum((csum - onehot) * onehot, axis=1)
    padded = (counts + tm - 1) // tm * tm
    ends = jnp.cumsum(padded)
    starts = ends - padded
    dest = (jnp.sum(onehot * starts[None, :], axis=1) + rank).astype(jnp.int32)
    tile_row0 = jnp.arange(n_tiles_max, dtype=jnp.int32) * tm
    tile_expert = jnp.minimum(jnp.sum((tile_row0[:, None] >= ends[None, :]).astype(jnp.int32), axis=1),
                              N_EXPERTS - 1).astype(jnp.int32)
    n_tiles = (ends[-1] // tm).astype(jnp.int32).reshape(1)
    token = jnp.arange(2 * m, dtype=jnp.int32) % m
    row_token = jnp.zeros((n_tiles_max * tm,), jnp.int32).at[dest].set(token, unique_indices=True)
    return dest, tile_expert, n_tiles, row_token


def _gather_rows_body(idx_ref, src_ref, o_ref, sem, *, tile):
    base = pl.program_id(0) * tile

    def start(r, carry):
        pltpu.make_async_copy(src_ref.at[idx_ref[base + r]], o_ref.at[r], sem).start()
        return carry

    lax.fori_loop(0, tile, start, 0, unroll=8)
    pltpu.make_async_copy(src_ref.at[pl.ds(0, tile)], o_ref, sem).wait()


def gather_rows(src3, idx, tile=1024):
    n = idx.shape[0]
    slab = src3.shape[1:]
    return pl.pallas_call(
        functools.partial(_gather_rows_body, tile=tile),
        grid_spec=pltpu.PrefetchScalarGridSpec(
            num_scalar_prefetch=1, grid=(n // tile,),
            in_specs=[pl.BlockSpec(memory_space=pl.ANY)],
            out_specs=pl.BlockSpec((tile,) + slab, lambda i, idx_ref: (i, 0, 0)),
            scratch_shapes=[pltpu.SemaphoreType.DMA(())]),
        out_shape=jax.ShapeDtypeStruct((n,) + slab, src3.dtype),
        compiler_params=_cparams(("arbitrary",)),
    )(idx, src3)


def _live_tile(t, nt_ref):
    return jnp.minimum(t, nt_ref[0] - 1)


def _moe_gu_body(te_ref, nt_ref, a_ref, bg_ref, bu_ref, o_ref):
    t = pl.program_id(1)

    @pl.when(t < nt_ref[0])
    def _():
        a = a_ref[...]
        g = _dot(a, bg_ref[...])
        u = _dot(a, bu_ref[...])
        o_ref[...] = (g * jax.nn.sigmoid(g) * u).astype(o_ref.dtype)

    @pl.when(t >= nt_ref[0])
    def _():
        o_ref[...] = jnp.zeros_like(o_ref)


def moe_gate_up(xs, w_gu, tile_expert, n_tiles, tm, tn=512):
    rows, k = xs.shape
    f = w_gu.shape[2] // 2
    nj = f // tn
    return pl.pallas_call(
        _moe_gu_body,
        grid_spec=pltpu.PrefetchScalarGridSpec(
            num_scalar_prefetch=2, grid=(nj, rows // tm),
            in_specs=[pl.BlockSpec((tm, k), lambda j, t, te, nt: (_live_tile(t, nt), 0)),
                      pl.BlockSpec((None, k, tn), lambda j, t, te, nt: (te[_live_tile(t, nt)], 0, j)),
                      pl.BlockSpec((None, k, tn), lambda j, t, te, nt: (te[_live_tile(t, nt)], 0, nj + j))],
            out_specs=pl.BlockSpec((tm, tn), lambda j, t, te, nt: (t, j))),
        out_shape=jax.ShapeDtypeStruct((rows, f), BF16),
        compiler_params=_cparams(("arbitrary", "arbitrary")),
    )(tile_expert, n_tiles, xs, w_gu, w_gu)


def _moe_down_body(te_ref, nt_ref, a_ref, b_ref, o_ref):
    t = pl.program_id(1)

    @pl.when(t < nt_ref[0])
    def _():
        o_ref[...] = _dot(a_ref[...], b_ref[...]).astype(o_ref.dtype)

    @pl.when(t >= nt_ref[0])
    def _():
        o_ref[...] = jnp.zeros_like(o_ref)


def moe_down(act, w_down, tile_expert, n_tiles, tm, tn=1024):
    rows, f = act.shape
    n = w_down.shape[2]
    return pl.pallas_call(
        _moe_down_body,
        grid_spec=pltpu.PrefetchScalarGridSpec(
            num_scalar_prefetch=2, grid=(n // tn, rows // tm),
            in_specs=[pl.BlockSpec((tm, f), lambda j, t, te, nt: (_live_tile(t, nt), 0)),
                      pl.BlockSpec((None, f, tn), lambda j, t, te, nt: (te[_live_tile(t, nt)], 0, j))],
            out_specs=pl.BlockSpec((tm, tn), lambda j, t, te, nt: (t, j))),
        out_shape=jax.ShapeDtypeStruct((rows, n), BF16),
        compiler_params=_cparams(("arbitrary", "arbitrary")),
    )(tile_expert, n_tiles, act, w_down)


def _combine_body(x_ref, y1_ref, y2_ref, w_ref, o_ref):
    w = w_ref[...]
    o_ref[...] = (x_ref[...] + _lane_col(w, 0) * y1_ref[...].astype(F32)
                  + _lane_col(w, 1) * y2_ref[...].astype(F32))


def moe_combine(x, yg, wts, tm=256):
    m, d = x.shape
    nb = m // tm
    return pl.pallas_call(
        _combine_body,
        grid=(nb,),
        in_specs=[pl.BlockSpec((tm, d), lambda i: (i, 0)), pl.BlockSpec((tm, d), lambda i: (i, 0)),
                  pl.BlockSpec((tm, d), lambda i: (nb + i, 0)), pl.BlockSpec((tm, LANES), lambda i: (i, 0))],
        out_specs=pl.BlockSpec((tm, d), lambda i: (i, 0)),
        out_shape=jax.ShapeDtypeStruct((m, d), F32),
        compiler_params=_cparams(("parallel",)),
    )(x, yg, yg, wts)


def moe_ffn(x, g, router, w_gu, w_down, tm=MOE_TM):
    m, d = x.shape
    h, idx, wts = norm_and_route(x, g, router)
    dest, tile_expert, n_tiles, row_token = _route_plan(idx, tm)
    rows = row_token.shape[0]
    xs3 = gather_rows(h.reshape(m, ROW_SLABS, d // ROW_SLABS), row_token)
    act = moe_gate_up(xs3.reshape(rows, d), w_gu, tile_expert, n_tiles, tm)
    ys = moe_down(act, w_down, tile_expert, n_tiles, tm)
    yg3 = gather_rows(ys.reshape(rows, ROW_SLABS, d // ROW_SLABS), dest)
    return moe_combine(x, yg3.reshape(2 * m, d), wts)


def _w_in_pieces():
    a_w = BRANCH_W
    o_if = 3 * a_w + 4 * a_w
    o_sg = o_if + 2 * N_HEADS
    o_swk = o_sg + 2 * a_w + SW_QHEADS * SW_DIM
    o_gate = o_swk + 2 * SW_KVHEADS * SW_DIM
    n_in = o_gate + N_BRANCH * D_MODEL
    main = ((0, o_if), (o_sg, o_swk), (o_gate, n_in))
    tail = ((o_swk, o_gate), (o_if, o_sg))
    return main, tail


def _split_w_in(w_in):
    main_p, tail_p = _w_in_pieces()
    main = jnp.concatenate([w_in[:, lo:hi] for lo, hi in main_p], axis=1).astype(BF16)
    pad = jnp.zeros((w_in.shape[0], N_TAIL - sum(hi - lo for lo, hi in tail_p)), w_in.dtype)
    tail = jnp.concatenate([w_in[:, lo:hi] for lo, hi in tail_p] + [pad], axis=1).astype(BF16)
    return main, tail


def _w_in_cast(w_in, block_rows=16):
    main_p, tail_p = _w_in_pieces()
    return SideCast(w_in, block_rows, ((N_MAIN, main_p), (N_TAIL, tail_p)))


def _mixer(h, w_main, w_tail, layer_params, batch, seq, sides=(), tm=1024):
    (conv_w, conv_b, gate_b, head_norm, sg_w, sg_b, q_norm, k_norm, sinks, w_branch, w_out) = layer_params
    proj, side_outs = inproj_main(h, w_main, tm=tm, sides=sides)
    tail = inproj_tail(h, w_tail)
    y_a = sb_attention(proj, batch, seq)
    y_b = mlstm(proj, tail, conv_w, conv_b, gate_b, head_norm, batch, seq)
    y_c = spatial_gating(proj, sg_w, sg_b)
    y_d = swa_attention(proj, tail, q_norm, k_norm, sinks, seq)
    merged = gated_merge((y_a, y_b, y_c, y_d), w_branch.astype(BF16), proj)
    return merged, w_out.astype(BF16), side_outs


def kernel(x, norm_mix, w_in, ml_conv_w, ml_conv_b, ml_gate_b, ml_head_norm, sg_w, sg_b, sw_q_norm, sw_k_norm,
           sw_sinks, w_branch, w_out, norm_ffn, ffn_w_gu, ffn_w_down, moe_router, moe_w_gu, moe_w_down):
    batch, seq, d = x.shape
    depth = norm_mix.shape[0]
    xf = x.reshape(batch * seq, d)
    w_main, w_tail = _split_w_in(w_in[0])
    moe_down_b = None
    for layer in range(depth):
        j = layer // 2
        dense = layer % 2 == 0
        has_next = layer + 1 < depth
        params = (ml_conv_w[layer], ml_conv_b[layer], ml_gate_b[layer], ml_head_norm[layer], sg_w[layer],
                  sg_b[layer], sw_q_norm[layer], sw_k_norm[layer], sw_sinks[layer], w_branch[layer], w_out[layer])
        h = rmsnorm(xf, norm_mix[layer])
        if dense:
            sides = (_plain_cast(ffn_w_gu[j], 16), _plain_cast(ffn_w_down[j], 64))
            merged, w_o, (w_gu_b, w_down_b) = _mixer(h, w_main, w_tail, params, batch, seq, sides)
        else:
            e, k, f2 = moe_w_gu[j].shape
            sides = (_plain_cast(moe_w_gu[j].reshape(e * k, f2), 64),)
            merged, w_o, (moe_gu_b,) = _mixer(h, w_main, w_tail, params, batch, seq, sides, tm=512)
            moe_gu_b = moe_gu_b.reshape(e, k, f2)
        xf = matmul_residual(merged, w_o, xf)
        if dense:
            h = rmsnorm(xf, norm_ffn[layer])
            if has_next:
                e, f, n = moe_w_down[j].shape
                sides = (_w_in_cast(w_in[layer + 1]), _plain_cast(moe_w_down[j].reshape(e * f, n), 64))
                act, (w_main, w_tail, moe_down_b) = ffn_gate_up(h, w_gu_b, sides=sides)
                moe_down_b = moe_down_b.reshape(e, f, n)
            else:
                act, _ = ffn_gate_up(h, w_gu_b)
            xf = ffn_down(act, w_down_b, xf)
        else:
            if moe_down_b is None:
                moe_down_b = moe_w_down[j].astype(BF16)
            xf = moe_ffn(xf, norm_ffn[layer], moe_router[j], moe_gu_b, moe_down_b)
            moe_down_b = None
            if has_next:
                w_main, w_tail = _split_w_in(w_in[layer + 1])
    return xf.reshape(batch, seq, d)
```

```python
import functools
import math
from typing import NamedTuple, Optional

import jax
import jax.numpy as jnp
from jax import lax
from jax.experimental import pallas as pl
from jax.experimental.pallas import tpu as pltpu

F32 = jnp.float32
BF16 = jnp.bfloat16
EPS = 1e-6
LANES = 128
CHUNK = 128
VMEM_LIMIT_MB = 56

D_MODEL = 4096
N_HEADS = 8
HEAD_DIM = 128
BRANCH_W = N_HEADS * HEAD_DIM
SW_QHEADS, SW_KVHEADS, SW_DIM = 16, 2, 64
N_BRANCH = 4
N_EXPERTS = 8

CB_SB_Q, CB_SB_K, CB_SB_V = 0, 1, 2
CB_ML_Q, CB_ML_K, CB_ML_V, CB_ML_O = 3, 4, 5, 6
CB_SG_U, CB_SG_V = 7, 8
CB_SW_Q = 9
CB_GATE0 = 10
N_MAIN = (CB_GATE0 + N_BRANCH * D_MODEL // BRANCH_W) * BRANCH_W
N_TAIL = 3 * LANES


def _cparams(sem, vmem_mb=VMEM_LIMIT_MB):
    return pltpu.CompilerParams(dimension_semantics=sem, vmem_limit_bytes=vmem_mb * 1024 * 1024)


def _dot(a, b):
    return jnp.dot(a, b, preferred_element_type=F32)


def _dot_nt(a, b):
    return lax.dot_general(a, b, (((1,), (1,)), ((), ())), preferred_element_type=F32)


def _dot_tn(a, b):
    return lax.dot_general(a, b, (((0,), (0,)), ((), ())), preferred_element_type=F32)


def _dot_f32(a, b):
    return jnp.dot(a, b, preferred_element_type=F32, precision=lax.Precision.HIGHEST)


def _lane_col(x, idx):
    lane = lax.broadcasted_iota(jnp.int32, x.shape, 1)
    return jnp.sum(jnp.where(lane == idx, x, 0.0), axis=1, keepdims=True)


def _softplus(z):
    return jnp.maximum(z, 0.0) + jnp.log(1.0 + jnp.exp(-jnp.abs(z)))


class SideCast(NamedTuple):
    src: jax.Array
    block_rows: int
    outs: tuple
    lead: Optional[int] = None


def _plain_cast(src, block_rows, lead=None):
    cols = src.shape[-1]
    return SideCast(src, block_rows, ((cols, ((0, cols),)),), lead)


def _hosted_call(body, grid, in_specs, out_specs, out_shape, args, sides):
    n_in, n_out, n_side = len(in_specs), len(out_specs), len(sides)
    steps = math.prod(grid)

    def linear(*ids):
        s = ids[0]
        for g, i in zip(grid[1:], ids[1:]):
            s = s * g + i
        return s

    in_specs, out_specs, out_shape, args = list(in_specs), list(out_specs), list(out_shape), list(args)
    for sc in sides:
        rows, cols = sc.src.shape[-2:]
        br = sc.block_rows
        while rows % br or rows // br > steps:
            br += 16
        nblk = rows // br

        def imap(*ids, nblk=nblk):
            return (jnp.minimum(linear(*ids), nblk - 1), 0)

        if sc.lead is None:
            in_specs.append(pl.BlockSpec((br, cols), imap))
        else:
            in_specs.append(pl.BlockSpec((None, br, cols), lambda *ids, imap=imap, lead=sc.lead: (lead,) + imap(*ids)))
        args.append(sc.src)
        for width, _ in sc.outs:
            out_specs.append(pl.BlockSpec((br, width), imap))
            out_shape.append(jax.ShapeDtypeStruct((rows, width), BF16))

    def hosted(*refs):
        outs = refs[n_in + n_side:]
        body(*refs[:n_in], *outs[:n_out])
        k = n_out
        for s_ref, sc in zip(refs[n_in:n_in + n_side], sides):
            v = s_ref[...]
            for width, pieces in sc.outs:
                dst = outs[k]
                k += 1
                off = 0
                for lo, hi in pieces:
                    dst[:, off:off + hi - lo] = v[:, lo:hi].astype(BF16)
                    off += hi - lo
                if off < width:
                    dst[:, off:width] = jnp.zeros((v.shape[0], width - off), BF16)

    res = pl.pallas_call(
        hosted, grid=grid, in_specs=in_specs, out_specs=out_specs, out_shape=out_shape,
        compiler_params=_cparams(("arbitrary",) * len(grid)),
    )(*args)
    return res[:n_out], res[n_out:]


def cast_weights(sides):
    steps = max(sc.src.shape[-2] // sc.block_rows for sc in sides)
    return _hosted_call(lambda: None, (steps,), [], [], [], (), sides)[1]


def _rmsnorm_body(x_ref, g_ref, o_ref):
    x = x_ref[...]
    ms = jnp.mean(x * x, axis=-1, keepdims=True)
    o_ref[...] = (x * lax.rsqrt(ms + EPS) * g_ref[...]).astype(o_ref.dtype)


def rmsnorm(x, g, tm=256):
    m, d = x.shape
    return pl.pallas_call(
        _rmsnorm_body,
        grid=(m // tm,),
        in_specs=[pl.BlockSpec((tm, d), lambda i: (i, 0)), pl.BlockSpec((1, d), lambda i: (0, 0))],
        out_specs=pl.BlockSpec((tm, d), lambda i: (i, 0)),
        out_shape=jax.ShapeDtypeStruct((m, d), BF16),
        compiler_params=_cparams(("parallel",)),
    )(x, g.reshape(1, d))


def _mm_body(a_ref, b_ref, o_ref):
    o_ref[...] = _dot(a_ref[...], b_ref[...]).astype(o_ref.dtype)


def inproj_main(h, w, tm=1024, tn=1024, sides=()):
    m, k = h.shape
    n = w.shape[1]
    (proj,), side_outs = _hosted_call(
        _mm_body, (n // tn, m // tm),
        [pl.BlockSpec((tm, k), lambda j, i: (i, 0)), pl.BlockSpec((k, tn), lambda j, i: (0, j))],
        [pl.BlockSpec((tm, tn), lambda j, i: (i, j))],
        [jax.ShapeDtypeStruct((m, n), BF16)], (h, w), sides)
    return proj, side_outs


def inproj_tail(h, w, tm=1024):
    m, k = h.shape
    n = w.shape[1]
    return pl.pallas_call(
        _mm_body,
        grid=(m // tm,),
        in_specs=[pl.BlockSpec((tm, k), lambda i: (i, 0)), pl.BlockSpec((k, n), lambda i: (0, 0))],
        out_specs=pl.BlockSpec((tm, n), lambda i: (i, 0)),
        out_shape=jax.ShapeDtypeStruct((m, n), F32),
        compiler_params=_cparams(("parallel",)),
    )(h, w)


def _mm_res_body(a_ref, b_ref, x_ref, o_ref):
    o_ref[...] = x_ref[...] + _dot(a_ref[...], b_ref[...])


def matmul_residual(a, w, x, tm=512, tn=1024):
    m, k = a.shape
    n = w.shape[1]
    return pl.pallas_call(
        _mm_res_body,
        grid=(n // tn, m // tm),
        in_specs=[pl.BlockSpec((tm, k), lambda j, i: (i, 0)), pl.BlockSpec((k, tn), lambda j, i: (0, j)),
                  pl.BlockSpec((tm, tn), lambda j, i: (i, j))],
        out_specs=pl.BlockSpec((tm, tn), lambda j, i: (i, j)),
        out_shape=jax.ShapeDtypeStruct((m, n), F32),
        compiler_params=_cparams(("parallel", "parallel")),
    )(a, w, x)


def _merge_body(ya_ref, yb_ref, yc_ref, yd_ref, wa_ref, wb_ref, wc_ref, wd_ref,
                ga_ref, gb_ref, gc_ref, gd_ref, o_ref):
    def gate(g_ref):
        return jax.nn.sigmoid(g_ref[...].astype(F32))

    acc = gate(ga_ref) * _dot(ya_ref[...], wa_ref[...])
    acc = acc + gate(gb_ref) * _dot(yb_ref[...], wb_ref[...])
    acc = acc + gate(gc_ref) * _dot(yc_ref[...], wc_ref[...])
    acc = acc + gate(gd_ref) * _dot(yd_ref[...], wd_ref[...])
    o_ref[...] = acc.astype(o_ref.dtype)


def gated_merge(ys, w_branch, proj, tm=512, tn=1024):
    m, kb = ys[0].shape
    d = w_branch.shape[2]
    gpb = d // tn
    g0 = CB_GATE0 * BRANCH_W // tn
    y_specs = [pl.BlockSpec((tm, kb), lambda j, i: (i, 0)) for _ in range(N_BRANCH)]
    w_specs = [pl.BlockSpec((None, kb, tn), functools.partial(lambda j, i, br: (br, 0, j), br=br))
               for br in range(N_BRANCH)]
    g_specs = [pl.BlockSpec((tm, tn), functools.partial(lambda j, i, br: (i, g0 + br * gpb + j), br=br))
               for br in range(N_BRANCH)]
    return pl.pallas_call(
        _merge_body,
        grid=(d // tn, m // tm),
        in_specs=y_specs + w_specs + g_specs,
        out_specs=pl.BlockSpec((tm, tn), lambda j, i: (i, j)),
        out_shape=jax.ShapeDtypeStruct((m, d), BF16),
        compiler_params=_cparams(("parallel", "parallel")),
    )(*ys, w_branch, w_branch, w_branch, w_branch, proj, proj, proj, proj)


def _sb_body(q_ref, k_ref, v_ref, o_ref, *, tq, tk, scale):
    qi = pl.program_id(2)
    r = tq // tk
    q = q_ref[...]
    ur = lax.broadcasted_iota(jnp.int32, (tk, 2 * tk), 0)
    uc = lax.broadcasted_iota(jnp.int32, (tk, 2 * tk), 1)
    cum_w = jnp.where((uc >= tk) | (ur > uc), 1.0, 0.0).astype(BF16)

    def key_block(start, acc, c, mask):
        k = k_ref[pl.ds(start, tq), :]
        v = v_ref[pl.ds(start, tq), :]
        z = _dot_nt(q, k) * scale
        nz = -z
        lg = jnp.log(1.0 + jnp.exp(jnp.minimum(z, nz)))
        l1mb = jnp.minimum(nz, 0.0) - lg
        log_beta = l1mb + z
        if mask is not None:
            l1mb = jnp.where(mask, l1mb, 0.0)
        l1mb_b = l1mb.astype(BF16)
        ws = [None] * r
        for j in range(r - 1, -1, -1):
            sl = slice(j * tk, (j + 1) * tk)
            t = _dot(l1mb_b[:, sl], cum_w)
            w = jnp.exp(log_beta[:, sl] + t[:, :tk] + c)
            if mask is not None:
                w = jnp.where(mask[:, sl], w, 0.0)
            ws[j] = w.astype(BF16)
            c = c + t[:, tk:]
        acc = acc + _dot(jnp.concatenate(ws, axis=1), v)
        return acc, c

    acc = jnp.zeros((tq, HEAD_DIM), F32)
    c = jnp.zeros((tq, tk), F32)
    rows = lax.broadcasted_iota(jnp.int32, (tq, tq), 0)
    cols = lax.broadcasted_iota(jnp.int32, (tq, tq), 1)
    acc, c = key_block(pl.multiple_of(qi * tq, tq), acc, c, cols < rows)

    def body(it, carry):
        return key_block(pl.multiple_of((qi - 1 - it) * tq, tq), *carry, None)

    acc, c = lax.fori_loop(0, qi, body, (acc, c))
    o_ref[...] = acc.astype(o_ref.dtype)


def sb_attention(proj, batch, seq, tq=512, tk=128):
    m = proj.shape[0]
    nq = seq // tq
    return pl.pallas_call(
        functools.partial(_sb_body, tq=tq, tk=tk, scale=HEAD_DIM ** -0.5),
        grid=(batch, N_HEADS, nq),
        in_specs=[pl.BlockSpec((tq, HEAD_DIM), lambda b, h, i: (b * nq + i, CB_SB_Q * N_HEADS + h)),
                  pl.BlockSpec((seq, HEAD_DIM), lambda b, h, i: (b, CB_SB_K * N_HEADS + h)),
                  pl.BlockSpec((seq, HEAD_DIM), lambda b, h, i: (b, CB_SB_V * N_HEADS + h))],
        out_specs=pl.BlockSpec((tq, HEAD_DIM), lambda b, h, i: (b * nq + i, h)),
        out_shape=jax.ShapeDtypeStruct((m, BRANCH_W), BF16),
        compiler_params=_cparams(("parallel", "parallel", "arbitrary")),
    )(proj, proj, proj)


def _mlstm_body(q_ref, k_ref, v_ref, o_ref, g_ref, cw_ref, cb_ref, gb_ref, hn_ref, y_ref,
                xs_ref, c_ref, n_ref, m_ref, *, n_conv):
    L = CHUNK
    chunk = pl.program_id(1)

    @pl.when(chunk == 0)
    def _():
        xs_ref[0:8, :] = jnp.zeros((8, 2 * BRANCH_W), F32)
        c_ref[...] = jnp.zeros_like(c_ref)
        n_ref[...] = jnp.zeros_like(n_ref)
        m_ref[...] = jnp.zeros_like(m_ref)

    xs_ref[8:8 + L, 0:BRANCH_W] = q_ref[...].astype(F32)
    xs_ref[8:8 + L, BRANCH_W:2 * BRANCH_W] = k_ref[...].astype(F32)
    conv = jnp.zeros((L, 2 * BRANCH_W), F32) + cb_ref[...]
    for j in range(n_conv):
        conv = conv + cw_ref[j:j + 1, :] * xs_ref[pl.ds(8 - (n_conv - 1) + j, L), :]
    xs_ref[0:8, :] = xs_ref[L:L + 8, :]
    qk = conv * jax.nn.sigmoid(conv)
    qf = qk[:, 0:BRANCH_W]
    kf = qk[:, BRANCH_W:2 * BRANCH_W] * (HEAD_DIM ** -0.5)

    g = g_ref[...] + gb_ref[...]
    lf = -_softplus(-g)
    ri = lax.broadcasted_iota(jnp.int32, (L, L), 0)
    ci = lax.broadcasted_iota(jnp.int32, (L, L), 1)
    tril = ri >= ci
    tril_f = jnp.where(tril, 1.0, 0.0).astype(F32)
    triu_f = jnp.where(ri <= ci, 1.0, 0.0).astype(F32)
    b_cols = _dot_f32(tril_f, lf)
    g_t = g.T
    b_rows = _dot_f32(lf.T, triu_f)

    for h in range(N_HEADS):
        sl = slice(h * HEAD_DIM, (h + 1) * HEAD_DIM)
        qh = qf[:, sl]
        kh = kf[:, sl]
        qb = qh.astype(BF16)
        kb = kh.astype(BF16)
        vb = v_ref[:, sl]
        i_col = _lane_col(g, h)
        b_col = _lane_col(b_cols, N_HEADS + h)
        i_row = g_t[h:h + 1, :]
        b_row = b_rows[N_HEADS + h:N_HEADS + h + 1, :]
        m_prev = m_ref[h:h + 1, 0:1]
        c_prev = c_ref[h]
        n_prev = n_ref[h:h + 1, :]

        log_d = jnp.where(tril, b_col - b_row + i_row, -jnp.inf)
        m_inter = b_col + m_prev
        m_t = jnp.maximum(m_inter, jnp.max(log_d, axis=1, keepdims=True))
        dmat = jnp.exp(log_d - m_t)
        s = _dot_nt(qb, kb) * dmat
        inter = jnp.exp(m_inter - m_t)
        num = _dot(s.astype(BF16), vb) + inter * _dot(qb, c_prev.astype(BF16))
        den = jnp.sum(s, axis=1, keepdims=True) + inter * jnp.sum(qb.astype(F32) * n_prev, axis=1, keepdims=True)
        hh = num / jnp.maximum(jnp.abs(den), jnp.exp(-m_t))

        m_new = m_t[L - 1:L, :]
        b_last = b_col[L - 1:L, :]
        w_col = jnp.exp(b_last - b_col + i_col - m_new)
        decay = jnp.exp(b_last + m_prev - m_new)
        kw = kh * w_col
        c_ref[h] = decay * c_prev + _dot_tn(kw.astype(BF16), vb)
        n_ref[h:h + 1, :] = decay * n_prev + jnp.sum(kw, axis=0, keepdims=True)
        m_ref[h:h + 1, :] = jnp.broadcast_to(m_new, (1, LANES))

        hn = hh * lax.rsqrt(jnp.mean(hh * hh, axis=1, keepdims=True) + EPS) * hn_ref[:, sl]
        y_ref[:, sl] = (jax.nn.sigmoid(o_ref[:, sl].astype(F32)) * hn).astype(y_ref.dtype)


def mlstm(proj, tail, conv_w, conv_b, gate_b, head_norm, batch, seq):
    m = proj.shape[0]
    nc = seq // CHUNK
    n_conv = conv_w.shape[0]
    gb = jnp.zeros((1, LANES), F32).at[0, 0:2 * N_HEADS].set(gate_b.reshape(-1))

    def blk(cb):
        return pl.BlockSpec((CHUNK, BRANCH_W), lambda b, c: (b * nc + c, cb))

    def const(shape):
        return pl.BlockSpec(shape, lambda b, c: (0, 0))

    return pl.pallas_call(
        functools.partial(_mlstm_body, n_conv=n_conv),
        grid=(batch, nc),
        in_specs=[blk(CB_ML_Q), blk(CB_ML_K), blk(CB_ML_V), blk(CB_ML_O),
                  pl.BlockSpec((CHUNK, LANES), lambda b, c: (b * nc + c, 2)),
                  const((n_conv, 2 * BRANCH_W)), const((1, 2 * BRANCH_W)), const((1, LANES)),
                  const((1, BRANCH_W))],
        out_specs=pl.BlockSpec((CHUNK, BRANCH_W), lambda b, c: (b * nc + c, 0)),
        out_shape=jax.ShapeDtypeStruct((m, BRANCH_W), BF16),
        scratch_shapes=[pltpu.VMEM((CHUNK + 8, 2 * BRANCH_W), F32),
                        pltpu.VMEM((N_HEADS, HEAD_DIM, HEAD_DIM), F32),
                        pltpu.VMEM((N_HEADS, HEAD_DIM), F32),
                        pltpu.VMEM((N_HEADS, LANES), F32)],
        compiler_params=_cparams(("parallel", "arbitrary")),
    )(proj, proj, proj, proj, tail, conv_w, conv_b.reshape(1, -1), gb, head_norm.reshape(1, -1))


def _gelu(x):
    return 0.5 * x * (1.0 + lax.erf(x * (2.0 ** -0.5)))


def _sgu_body(u_ref, v_ref, w_ref, b_ref, o_ref):
    L = CHUNK
    u = _gelu(u_ref[...].astype(F32))
    v = _gelu(v_ref[...].astype(F32))
    mu = jnp.mean(v, axis=1, keepdims=True)
    vc = v - mu
    var = jnp.mean(vc * vc, axis=1, keepdims=True)
    vn = (vc * lax.rsqrt(var + EPS)).astype(BF16)
    ri = lax.broadcasted_iota(jnp.int32, (L, L), 0)
    ci = lax.broadcasted_iota(jnp.int32, (L, L), 1)
    tril = ri >= ci
    bias = b_ref[...]
    for g in range(N_HEADS):
        sl = slice(g * HEAD_DIM, (g + 1) * HEAD_DIM)
        wg = jnp.where(tril, w_ref[g], 0.0).astype(BF16)
        mixed = _dot(wg, vn[:, sl]) + _lane_col(bias, g)
        o_ref[:, sl] = (u[:, sl] * mixed).astype(o_ref.dtype)


def spatial_gating(proj, sg_w, sg_b):
    m = proj.shape[0]
    bias = jnp.zeros((CHUNK, LANES), F32).at[:, 0:N_HEADS].set(sg_b.T)
    return pl.pallas_call(
        _sgu_body,
        grid=(m // CHUNK,),
        in_specs=[pl.BlockSpec((CHUNK, BRANCH_W), lambda i: (i, CB_SG_U)),
                  pl.BlockSpec((CHUNK, BRANCH_W), lambda i: (i, CB_SG_V)),
                  pl.BlockSpec((N_HEADS, CHUNK, CHUNK), lambda i: (0, 0, 0)),
                  pl.BlockSpec((CHUNK, LANES), lambda i: (0, 0))],
        out_specs=pl.BlockSpec((CHUNK, BRANCH_W), lambda i: (i, 0)),
        out_shape=jax.ShapeDtypeStruct((m, BRANCH_W), BF16),
        compiler_params=_cparams(("parallel",)),
    )(proj, proj, sg_w, bias)


def _swa_body(q_ref, kp_ref, kc_ref, vp_ref, vc_ref, qg_ref, kg_ref, sink_ref, o_ref, *, n_chunks):
    W = CHUNK
    n = pl.program_id(0) % n_chunks
    lane = lax.broadcasted_iota(jnp.int32, (1, LANES), 1)
    lo = lane < SW_DIM

    def pair_norm(x, gain):
        x2 = x * x
        s_lo = jnp.sum(jnp.where(lo, x2, 0.0), axis=1, keepdims=True)
        s_hi = jnp.sum(jnp.where(lo, 0.0, x2), axis=1, keepdims=True)
        ms = jnp.where(lo, s_lo, s_hi) * (1.0 / SW_DIM)
        return x * lax.rsqrt(ms + EPS) * gain

    kg = kg_ref[...]
    kk = jnp.concatenate([pair_norm(kp_ref[...], kg), pair_norm(kc_ref[...], kg)], axis=0)
    vv = jnp.concatenate([vp_ref[...], vc_ref[...]], axis=0)
    kk_sw = pltpu.roll(kk, SW_DIM, axis=1)
    vv_sw = pltpu.roll(vv, SW_DIM, axis=1)
    k_ver = [[jnp.where(lo, kk, 0.0).astype(BF16), jnp.where(lo, 0.0, kk_sw).astype(BF16)],
             [jnp.where(lo, kk_sw, 0.0).astype(BF16), jnp.where(lo, 0.0, kk).astype(BF16)]]
    v_ver = [[jnp.where(lo, vv, 0.0).astype(BF16), jnp.where(lo, 0.0, vv_sw).astype(BF16)],
             [jnp.where(lo, vv_sw, 0.0).astype(BF16), jnp.where(lo, 0.0, vv).astype(BF16)]]

    stack = SW_QHEADS // SW_KVHEADS // 2
    qi = lax.broadcasted_iota(jnp.int32, (stack * W, 2 * W), 0) % W
    kj = lax.broadcasted_iota(jnp.int32, (stack * W, 2 * W), 1)
    diff = W + qi - kj
    valid = (diff >= 0) & (diff < W) & ((n > 0) | (kj >= W))
    scale = SW_DIM ** -0.5
    qg = qg_ref[...]
    qn = [pair_norm(q_ref[:, pb * LANES:(pb + 1) * LANES].astype(F32), qg) for pb in range(SW_QHEADS // 2)]
    for hk in range(SW_KVHEADS):
        outs = []
        for half in range(2):
            sel = lo if half == 0 else jnp.logical_not(lo)
            qs = jnp.concatenate([jnp.where(sel, qn[hk * stack + i], 0.0) for i in range(stack)], axis=0)
            s = jnp.where(valid, _dot_nt(qs.astype(BF16), k_ver[hk][half]) * scale, -jnp.inf)
            st = hk * 2 + half
            sink = sink_ref[st * stack * W:(st + 1) * stack * W, :]
            mx = jnp.maximum(jnp.max(s, axis=1, keepdims=True), sink)
            p = jnp.exp(s - mx)
            inv = 1.0 / (jnp.sum(p, axis=1, keepdims=True) + jnp.exp(sink - mx))
            outs.append(_dot(p.astype(BF16), v_ver[hk][half]) * inv)
        for i in range(stack):
            pb = hk * stack + i
            o_ref[:, pb * LANES:(pb + 1) * LANES] = (outs[0][i * W:(i + 1) * W]
                                                     + outs[1][i * W:(i + 1) * W]).astype(o_ref.dtype)


def swa_attention(proj, tail, q_norm, k_norm, sinks, seq):
    m = proj.shape[0]
    nc = seq // CHUNK
    qg = jnp.tile(q_norm.reshape(1, SW_DIM), (1, LANES // SW_DIM))
    kg = jnp.tile(k_norm.reshape(1, SW_DIM), (1, LANES // SW_DIM))
    stack = SW_QHEADS // SW_KVHEADS // 2
    order = [2 * (hk * stack + i) + half for hk in range(SW_KVHEADS) for half in range(2) for i in range(stack)]
    sink_rows = jnp.repeat(sinks.astype(F32)[jnp.array(order)], CHUNK).reshape(SW_QHEADS * CHUNK, 1)

    def prev(i):
        return jnp.maximum(i - 1, 0)

    return pl.pallas_call(
        functools.partial(_swa_body, n_chunks=nc),
        grid=(m // CHUNK,),
        in_specs=[pl.BlockSpec((CHUNK, BRANCH_W), lambda i: (i, CB_SW_Q)),
                  pl.BlockSpec((CHUNK, LANES), lambda i: (prev(i), 0)),
                  pl.BlockSpec((CHUNK, LANES), lambda i: (i, 0)),
                  pl.BlockSpec((CHUNK, LANES), lambda i: (prev(i), 1)),
                  pl.BlockSpec((CHUNK, LANES), lambda i: (i, 1)),
                  pl.BlockSpec((1, LANES), lambda i: (0, 0)),
                  pl.BlockSpec((1, LANES), lambda i: (0, 0)),
                  pl.BlockSpec((SW_QHEADS * CHUNK, 1), lambda i: (0, 0))],
        out_specs=pl.BlockSpec((CHUNK, BRANCH_W), lambda i: (i, 0)),
        out_shape=jax.ShapeDtypeStruct((m, BRANCH_W), BF16),
        compiler_params=_cparams(("parallel",)),
    )(proj, tail, tail, tail, tail, qg, kg, sink_rows)


def _ffn_gu_body(a_ref, bg_ref, bu_ref, o_ref):
    a = a_ref[...]
    g = _dot(a, bg_ref[...])
    u = _dot(a, bu_ref[...])
    o_ref[...] = (g * jax.nn.sigmoid(g) * u).astype(o_ref.dtype)


def ffn_gate_up(h, w_gu, tm=1024, tn=512, sides=()):
    m, k = h.shape
    f = w_gu.shape[1] // 2
    nj = f // tn
    (act,), side_outs = _hosted_call(
        _ffn_gu_body, (nj, m // tm),
        [pl.BlockSpec((tm, k), lambda j, i: (i, 0)),
         pl.BlockSpec((k, tn), lambda j, i: (0, j)),
         pl.BlockSpec((k, tn), lambda j, i: (0, nj + j))],
        [pl.BlockSpec((tm, tn), lambda j, i: (i, j))],
        [jax.ShapeDtypeStruct((m, f), BF16)], (h, w_gu, w_gu), sides)
    return act, side_outs


def _ffn_down_body(a_ref, b_ref, x_ref, o_ref):
    kk = pl.program_id(2)
    part = _dot(a_ref[...], b_ref[...])

    @pl.when(kk == 0)
    def _():
        o_ref[...] = x_ref[...] + part

    @pl.when(kk > 0)
    def _():
        o_ref[...] += part


def ffn_down(act, w_down, x, tm=1024, tn=1024, tk=3584):
    m, f = act.shape
    n = w_down.shape[1]
    nk = f // tk
    return pl.pallas_call(
        _ffn_down_body,
        grid=(n // tn, m // tm, nk),
        in_specs=[pl.BlockSpec((tm, tk), lambda j, i, k: (i, k)),
                  pl.BlockSpec((tk, tn), lambda j, i, k: (k, j)),
                  pl.BlockSpec((tm, tn), lambda j, i, k: (i, j))],
        out_specs=pl.BlockSpec((tm, tn), lambda j, i, k: (i, j)),
        out_shape=jax.ShapeDtypeStruct((m, n), F32),
        compiler_params=_cparams(("parallel", "parallel", "arbitrary")),
    )(act, w_down, x)


MOE_TM = 512
ROW_SLABS = D_MODEL // LANES


def _router_body(x_ref, g_ref, r_ref, h_ref, idx_ref, wts_ref):
    x = x_ref[...]
    ms = jnp.mean(x * x, axis=-1, keepdims=True)
    h = x * lax.rsqrt(ms + EPS) * g_ref[...]
    for s in range(ROW_SLABS):
        h_ref[:, s, :] = h[:, s * LANES:(s + 1) * LANES]
    logits = _dot_f32(h, r_ref[...])
    lane = lax.broadcasted_iota(jnp.int32, logits.shape, 1)
    logits = jnp.where(lane < N_EXPERTS, logits, -jnp.inf)
    m1 = jnp.max(logits, axis=1, keepdims=True)
    i1 = jnp.min(jnp.where(logits == m1, lane, LANES), axis=1, keepdims=True)
    rest = jnp.where(lane == i1, -jnp.inf, logits)
    m2 = jnp.max(rest, axis=1, keepdims=True)
    i2 = jnp.min(jnp.where(rest == m2, lane, LANES), axis=1, keepdims=True)
    e2 = jnp.exp(m2 - m1)
    w1 = 1.0 / (1.0 + e2)
    w2 = e2 / (1.0 + e2)
    idx_ref[...] = jnp.where(lane == 0, i1, jnp.where(lane == 1, i2, 0))
    wts_ref[...] = jnp.where(lane == 0, w1, jnp.where(lane == 1, w2, 0.0))


def norm_and_route(x, g, router, tm=256):
    m, d = x.shape
    r_pad = jnp.zeros((d, LANES), F32).at[:, 0:N_EXPERTS].set(router)
    return pl.pallas_call(
        _router_body,
        grid=(m // tm,),
        in_specs=[pl.BlockSpec((tm, d), lambda i: (i, 0)), pl.BlockSpec((1, d), lambda i: (0, 0)),
                  pl.BlockSpec((d, LANES), lambda i: (0, 0))],
        out_specs=[pl.BlockSpec((tm, ROW_SLABS, LANES), lambda i: (i, 0, 0)), pl.BlockSpec((tm, LANES), lambda i: (i, 0)),
                   pl.BlockSpec((tm, LANES), lambda i: (i, 0))],
        out_shape=[jax.ShapeDtypeStruct((m, ROW_SLABS, LANES), F32), jax.ShapeDtypeStruct((m, LANES), jnp.int32),
                   jax.ShapeDtypeStruct((m, LANES), F32)],
        compiler_params=_cparams(("parallel",)),
    )(x, g.reshape(1, d), r_pad)


def _route_plan(idx, tm):
    m = idx.shape[0]
    n_tiles_max = -(-(2 * m + N_EXPERTS * (tm - 1)) // tm)
    e_flat = jnp.concatenate([idx[:, 0], idx[:, 1]])
    onehot = (e_flat[:, None] == jnp.arange(N_EXPERTS, dtype=jnp.int32)[None, :]).astype(jnp.int32)
    csum = jnp.cumsum(onehot, axis=0)
    counts = csum[-1]
    rank = jnp.sum((csum - onehot) * onehot, axis=1)
    padded = (counts + tm - 1) // tm * tm
    ends = jnp.cumsum(padded)
    starts = ends - padded
    dest = (jnp.sum(onehot * starts[None, :], axis=1) + rank).astype(jnp.int32)
    tile_row0 = jnp.arange(n_tiles_max, dtype=jnp.int32) * tm
    tile_expert = jnp.minimum(jnp.sum((tile_row0[:, None] >= ends[None, :]).astype(jnp.int32), axis=1),
                              N_EXPERTS - 1).astype(jnp.int32)
    n_tiles = (ends[-1] // tm).astype(jnp.int32).reshape(1)
    token = jnp.arange(2 * m, dtype=jnp.int32) % m
    row_token = jnp.zeros((n_tiles_max * tm,), jnp.int32).at[dest].set(token, unique_indices=True)
    return dest, tile_expert, n_tiles, row_token


def _dispatch_body(idx_ref, src_ref, o_ref, buf, sem, *, tile):
    base = pl.program_id(0) * tile

    def start(r, carry):
        pltpu.make_async_copy(src_ref.at[idx_ref[base + r]], buf.at[r], sem).start()
        return carry

    lax.fori_loop(0, tile, start, 0, unroll=8)
    pltpu.make_async_copy(src_ref.at[pl.ds(0, tile)], buf, sem).wait()
    for s in range(ROW_SLABS):
        o_ref[:, s * LANES:(s + 1) * LANES] = buf[:, s, :].astype(o_ref.dtype)


def moe_dispatch(h3, row_token, tile=512):
    n = row_token.shape[0]
    slab = h3.shape[1:]
    return pl.pallas_call(
        functools.partial(_dispatch_body, tile=tile),
        grid_spec=pltpu.PrefetchScalarGridSpec(
            num_scalar_prefetch=1, grid=(n // tile,),
            in_specs=[pl.BlockSpec(memory_space=pl.ANY)],
            out_specs=pl.BlockSpec((tile, slab[0] * slab[1]), lambda i, idx_ref: (i, 0)),
            scratch_shapes=[pltpu.VMEM((tile,) + slab, h3.dtype), pltpu.SemaphoreType.DMA(())]),
        out_shape=jax.ShapeDtypeStruct((n, slab[0] * slab[1]), BF16),
        compiler_params=_cparams(("arbitrary",)),
    )(row_token, h3)


def _live_tile(t, nt_ref):
    return jnp.minimum(t, nt_ref[0] - 1)


def _moe_gu_body(te_ref, nt_ref, a_ref, bg_ref, bu_ref, o_ref):
    t = pl.program_id(1)

    @pl.when(t < nt_ref[0])
    def _():
        a = a_ref[...]
        g = _dot(a, bg_ref[...])
        u = _dot(a, bu_ref[...])
        o_ref[...] = (g * jax.nn.sigmoid(g) * u).astype(o_ref.dtype)

    @pl.when(t >= nt_ref[0])
    def _():
        o_ref[...] = jnp.zeros_like(o_ref)


def moe_gate_up(xs, w_gu, tile_expert, n_tiles, tm, tn=512):
    rows, k = xs.shape
    f = w_gu.shape[2] // 2
    nj = f // tn
    return pl.pallas_call(
        _moe_gu_body,
        grid_spec=pltpu.PrefetchScalarGridSpec(
            num_scalar_prefetch=2, grid=(nj, rows // tm),
            in_specs=[pl.BlockSpec((tm, k), lambda j, t, te, nt: (_live_tile(t, nt), 0)),
                      pl.BlockSpec((None, k, tn), lambda j, t, te, nt: (te[_live_tile(t, nt)], 0, j)),
                      pl.BlockSpec((None, k, tn), lambda j, t, te, nt: (te[_live_tile(t, nt)], 0, nj + j))],
            out_specs=pl.BlockSpec((tm, tn), lambda j, t, te, nt: (t, j))),
        out_shape=jax.ShapeDtypeStruct((rows, f), BF16),
        compiler_params=_cparams(("arbitrary", "arbitrary")),
    )(tile_expert, n_tiles, xs, w_gu, w_gu)


def _moe_down_body(te_ref, nt_ref, a_ref, b_ref, o_ref):
    t = pl.program_id(1)

    @pl.when(t < nt_ref[0])
    def _():
        y = _dot(a_ref[...], b_ref[...])
        for s in range(o_ref.shape[1]):
            o_ref[:, s, :] = y[:, s * LANES:(s + 1) * LANES]

    @pl.when(t >= nt_ref[0])
    def _():
        o_ref[...] = jnp.zeros_like(o_ref)


def moe_down(act, w_down, tile_expert, n_tiles, tm, tn=1024):
    rows, f = act.shape
    n = w_down.shape[2]
    return pl.pallas_call(
        _moe_down_body,
        grid_spec=pltpu.PrefetchScalarGridSpec(
            num_scalar_prefetch=2, grid=(n // tn, rows // tm),
            in_specs=[pl.BlockSpec((tm, f), lambda j, t, te, nt: (_live_tile(t, nt), 0)),
                      pl.BlockSpec((None, f, tn), lambda j, t, te, nt: (te[_live_tile(t, nt)], 0, j))],
            out_specs=pl.BlockSpec((tm, tn // LANES, LANES), lambda j, t, te, nt: (t, j, 0))),
        out_shape=jax.ShapeDtypeStruct((rows, n // LANES, LANES), F32),
        compiler_params=_cparams(("arbitrary", "arbitrary")),
    )(tile_expert, n_tiles, act, w_down)


def _collect_combine_body(pos_ref, x_ref, w_ref, ys_ref, o_ref, buf1, buf2, sem, *, tile, m_tokens):
    base = pl.program_id(0) * tile

    def start(r, carry):
        pltpu.make_async_copy(ys_ref.at[pos_ref[base + r]], buf1.at[r], sem).start()
        pltpu.make_async_copy(ys_ref.at[pos_ref[m_tokens + base + r]], buf2.at[r], sem).start()
        return carry

    lax.fori_loop(0, tile, start, 0, unroll=4)
    pltpu.make_async_copy(ys_ref.at[pl.ds(0, tile)], buf1, sem).wait()
    pltpu.make_async_copy(ys_ref.at[pl.ds(0, tile)], buf2, sem).wait()
    w = w_ref[...]
    w1 = _lane_col(w, 0)
    w2 = _lane_col(w, 1)
    for s in range(ROW_SLABS):
        sl = slice(s * LANES, (s + 1) * LANES)
        o_ref[:, sl] = x_ref[:, sl] + w1 * buf1[:, s, :] + w2 * buf2[:, s, :]


def moe_collect_combine(x, ys3, pos, wts, tile=256):
    m, d = x.shape
    slab = ys3.shape[1:]
    return pl.pallas_call(
        functools.partial(_collect_combine_body, tile=tile, m_tokens=m),
        grid_spec=pltpu.PrefetchScalarGridSpec(
            num_scalar_prefetch=1, grid=(m // tile,),
            in_specs=[pl.BlockSpec((tile, d), lambda i, pos_ref: (i, 0)),
                      pl.BlockSpec((tile, LANES), lambda i, pos_ref: (i, 0)),
                      pl.BlockSpec(memory_space=pl.ANY)],
            out_specs=pl.BlockSpec((tile, d), lambda i, pos_ref: (i, 0)),
            scratch_shapes=[pltpu.VMEM((tile,) + slab, ys3.dtype), pltpu.VMEM((tile,) + slab, ys3.dtype),
                            pltpu.SemaphoreType.DMA(())]),
        out_shape=jax.ShapeDtypeStruct((m, d), F32),
        compiler_params=_cparams(("arbitrary",)),
    )(pos, x, wts, ys3)


def moe_ffn(x, g, router, w_gu, w_down, tm=MOE_TM):
    m, d = x.shape
    h3, idx, wts = norm_and_route(x, g, router)
    dest, tile_expert, n_tiles, row_token = _route_plan(idx, tm)
    xs = moe_dispatch(h3, row_token)
    act = moe_gate_up(xs, w_gu, tile_expert, n_tiles, tm)
    ys3 = moe_down(act, w_down, tile_expert, n_tiles, tm)
    return moe_collect_combine(x, ys3, dest, wts)


def _w_in_pieces():
    a_w = BRANCH_W
    o_if = 3 * a_w + 4 * a_w
    o_sg = o_if + 2 * N_HEADS
    o_swk = o_sg + 2 * a_w + SW_QHEADS * SW_DIM
    o_gate = o_swk + 2 * SW_KVHEADS * SW_DIM
    n_in = o_gate + N_BRANCH * D_MODEL
    main = ((0, o_if), (o_sg, o_swk), (o_gate, n_in))
    tail = ((o_swk, o_gate), (o_if, o_sg))
    return main, tail


def _w_in_cast(w_in_all, layer, block_rows=16):
    main_p, tail_p = _w_in_pieces()
    return SideCast(w_in_all, block_rows, ((N_MAIN, main_p), (N_TAIL, tail_p)), layer)


def _branch_casts(w_branch, w_out, layer):
    n_layers, nb, kb, d = w_branch.shape
    return (_plain_cast(w_branch.reshape(n_layers, nb * kb, d), 16, layer), _plain_cast(w_out, 16, layer))


def _mixer(h, w_main, w_tail, layer_params, batch, seq, sides=(), tm=1024):
    (conv_w, conv_b, gate_b, head_norm, sg_w, sg_b, q_norm, k_norm, sinks) = layer_params
    proj, side_outs = inproj_main(h, w_main, tm=tm, sides=sides)
    tail = inproj_tail(h, w_tail)
    y_a = sb_attention(proj, batch, seq)
    y_b = mlstm(proj, tail, conv_w, conv_b, gate_b, head_norm, batch, seq)
    y_c = spatial_gating(proj, sg_w, sg_b)
    y_d = swa_attention(proj, tail, q_norm, k_norm, sinks, seq)
    return proj, (y_a, y_b, y_c, y_d), side_outs


def kernel(x, norm_mix, w_in, ml_conv_w, ml_conv_b, ml_gate_b, ml_head_norm, sg_w, sg_b, sw_q_norm, sw_k_norm,
           sw_sinks, w_branch, w_out, norm_ffn, ffn_w_gu, ffn_w_down, moe_router, moe_w_gu, moe_w_down):
    batch, seq, d = x.shape
    depth = norm_mix.shape[0]
    nb, kb = w_branch.shape[1:3]
    xf = x.reshape(batch * seq, d)
    w_main, w_tail = cast_weights((_w_in_cast(w_in, 0, 64),))
    w_br_b = w_o_b = moe_down_b = None
    for layer in range(depth):
        j = layer // 2
        dense = layer % 2 == 0
        has_next = layer + 1 < depth
        params = (ml_conv_w[layer], ml_conv_b[layer], ml_gate_b[layer], ml_head_norm[layer], sg_w[layer],
                  sg_b[layer], sw_q_norm[layer], sw_k_norm[layer], sw_sinks[layer])
        h = rmsnorm(xf, norm_mix[layer])
        if dense:
            sides = (_plain_cast(ffn_w_gu, 16, j), _plain_cast(ffn_w_down, 64, j)) + _branch_casts(w_branch, w_out, layer)
            proj, ys, (w_gu_b, w_down_b, w_br_b, w_o_b) = _mixer(h, w_main, w_tail, params, batch, seq, sides)
        else:
            n_moe, e, k, f2 = moe_w_gu.shape
            sides = (_plain_cast(moe_w_gu.reshape(n_moe, e * k, f2), 64, j),)
            proj, ys, (moe_gu_b,) = _mixer(h, w_main, w_tail, params, batch, seq, sides, tm=512)
            moe_gu_b = moe_gu_b.reshape(e, k, f2)
            if w_br_b is None:
                w_br_b, w_o_b = cast_weights(_branch_casts(w_branch, w_out, layer))
        merged = gated_merge(ys, w_br_b.reshape(nb, kb, d), proj)
        xf = matmul_residual(merged, w_o_b, xf)
        w_br_b = w_o_b = None
        if dense:
            h = rmsnorm(xf, norm_ffn[layer])
            if has_next:
                n_moe, e, f, n = moe_w_down.shape
                sides = ((_w_in_cast(w_in, layer + 1), _plain_cast(moe_w_down.reshape(n_moe, e * f, n), 64, j))
                         + _branch_casts(w_branch, w_out, layer + 1))
                act, (w_main, w_tail, moe_down_b, w_br_b, w_o_b) = ffn_gate_up(h, w_gu_b, sides=sides)
                moe_down_b = moe_down_b.reshape(e, f, n)
            else:
                act, _ = ffn_gate_up(h, w_gu_b)
            xf = ffn_down(act, w_down_b, xf)
        else:
            if moe_down_b is None:
                moe_down_b = moe_w_down[j].astype(BF16)
            xf = moe_ffn(xf, norm_ffn[layer], moe_router[j], moe_gu_b, moe_down_b)
            moe_down_b = None
            if has_next:
                w_main, w_tail = cast_weights((_w_in_cast(w_in, layer + 1, 64),))
    return xf.reshape(batch, seq, d)
```

```python
import functools
import math
from typing import NamedTuple, Optional

import jax
import jax.numpy as jnp
from jax import lax
from jax.experimental import pallas as pl
from jax.experimental.pallas import tpu as pltpu

F32 = jnp.float32
BF16 = jnp.bfloat16
EPS = 1e-6
LANES = 128
CHUNK = 128
VMEM_LIMIT_MB = 56

D_MODEL = 4096
N_HEADS = 8
HEAD_DIM = 128
BRANCH_W = N_HEADS * HEAD_DIM
SW_QHEADS, SW_KVHEADS, SW_DIM = 16, 2, 64
N_BRANCH = 4
N_EXPERTS = 8

CB_SB_Q, CB_SB_K, CB_SB_V = 0, 1, 2
CB_ML_Q, CB_ML_K, CB_ML_V, CB_ML_O = 3, 4, 5, 6
CB_SG_U, CB_SG_V = 7, 8
CB_SW_Q = 9
CB_GATE0 = 10
N_MAIN = (CB_GATE0 + N_BRANCH * D_MODEL // BRANCH_W) * BRANCH_W
N_TAIL = 3 * LANES


def _cparams(sem, vmem_mb=VMEM_LIMIT_MB):
    return pltpu.CompilerParams(dimension_semantics=sem, vmem_limit_bytes=vmem_mb * 1024 * 1024)


def _dot(a, b):
    return jnp.dot(a, b, preferred_element_type=F32)


def _dot_nt(a, b):
    return lax.dot_general(a, b, (((1,), (1,)), ((), ())), preferred_element_type=F32)


def _dot_tn(a, b):
    return lax.dot_general(a, b, (((0,), (0,)), ((), ())), preferred_element_type=F32)


def _dot_f32(a, b):
    return jnp.dot(a, b, preferred_element_type=F32, precision=lax.Precision.HIGHEST)


def _lane_col(x, idx):
    lane = lax.broadcasted_iota(jnp.int32, x.shape, 1)
    return jnp.sum(jnp.where(lane == idx, x, 0.0), axis=1, keepdims=True)


def _softplus(z):
    return jnp.maximum(z, 0.0) + jnp.log(1.0 + jnp.exp(-jnp.abs(z)))


class SideCast(NamedTuple):
    src: jax.Array
    block_rows: int
    outs: tuple
    lead: Optional[int] = None


def _plain_cast(src, block_rows, lead=None):
    cols = src.shape[-1]
    return SideCast(src, block_rows, ((cols, ((0, cols),)),), lead)


def _hosted_call(body, grid, in_specs, out_specs, out_shape, args, sides):
    n_in, n_out, n_side = len(in_specs), len(out_specs), len(sides)
    steps = math.prod(grid)

    def linear(*ids):
        s = ids[0]
        for g, i in zip(grid[1:], ids[1:]):
            s = s * g + i
        return s

    in_specs, out_specs, out_shape, args = list(in_specs), list(out_specs), list(out_shape), list(args)
    for sc in sides:
        rows, cols = sc.src.shape[-2:]
        br = sc.block_rows
        while rows % br or rows // br > steps:
            br += 16
        nblk = rows // br

        def imap(*ids, nblk=nblk):
            return (jnp.minimum(linear(*ids), nblk - 1), 0)

        if sc.lead is None:
            in_specs.append(pl.BlockSpec((br, cols), imap))
        else:
            in_specs.append(pl.BlockSpec((None, br, cols), lambda *ids, imap=imap, lead=sc.lead: (lead,) + imap(*ids)))
        args.append(sc.src)
        for width, _ in sc.outs:
            out_specs.append(pl.BlockSpec((br, width), imap))
            out_shape.append(jax.ShapeDtypeStruct((rows, width), BF16))

    def hosted(*refs):
        outs = refs[n_in + n_side:]
        body(*refs[:n_in], *outs[:n_out])
        k = n_out
        for s_ref, sc in zip(refs[n_in:n_in + n_side], sides):
            v = s_ref[...]
            for width, pieces in sc.outs:
                dst = outs[k]
                k += 1
                off = 0
                for lo, hi in pieces:
                    dst[:, off:off + hi - lo] = v[:, lo:hi].astype(BF16)
                    off += hi - lo
                if off < width:
                    dst[:, off:width] = jnp.zeros((v.shape[0], width - off), BF16)

    res = pl.pallas_call(
        hosted, grid=grid, in_specs=in_specs, out_specs=out_specs, out_shape=out_shape,
        compiler_params=_cparams(("arbitrary",) * len(grid)),
    )(*args)
    return res[:n_out], res[n_out:]


def cast_weights(sides):
    steps = max(sc.src.shape[-2] // sc.block_rows for sc in sides)
    return _hosted_call(lambda: None, (steps,), [], [], [], (), sides)[1]


def _rmsnorm_body(x_ref, g_ref, o_ref):
    x = x_ref[...]
    ms = jnp.mean(x * x, axis=-1, keepdims=True)
    o_ref[...] = (x * lax.rsqrt(ms + EPS) * g_ref[...]).astype(o_ref.dtype)


def rmsnorm(x, g, tm=256):
    m, d = x.shape
    return pl.pallas_call(
        _rmsnorm_body,
        grid=(m // tm,),
        in_specs=[pl.BlockSpec((tm, d), lambda i: (i, 0)), pl.BlockSpec((1, d), lambda i: (0, 0))],
        out_specs=pl.BlockSpec((tm, d), lambda i: (i, 0)),
        out_shape=jax.ShapeDtypeStruct((m, d), BF16),
        compiler_params=_cparams(("parallel",)),
    )(x, g.reshape(1, d))


def _mm_body(a_ref, b_ref, o_ref):
    o_ref[...] = _dot(a_ref[...], b_ref[...]).astype(o_ref.dtype)


def inproj_main(h, w, tm=1024, tn=1024, sides=()):
    m, k = h.shape
    n = w.shape[1]
    (proj,), side_outs = _hosted_call(
        _mm_body, (n // tn, m // tm),
        [pl.BlockSpec((tm, k), lambda j, i: (i, 0)), pl.BlockSpec((k, tn), lambda j, i: (0, j))],
        [pl.BlockSpec((tm, tn), lambda j, i: (i, j))],
        [jax.ShapeDtypeStruct((m, n), BF16)], (h, w), sides)
    return proj, side_outs


def inproj_tail(h, w, tm=1024):
    m, k = h.shape
    n = w.shape[1]
    return pl.pallas_call(
        _mm_body,
        grid=(m // tm,),
        in_specs=[pl.BlockSpec((tm, k), lambda i: (i, 0)), pl.BlockSpec((k, n), lambda i: (0, 0))],
        out_specs=pl.BlockSpec((tm, n), lambda i: (i, 0)),
        out_shape=jax.ShapeDtypeStruct((m, n), F32),
        compiler_params=_cparams(("parallel",)),
    )(h, w)


def _mm_res_body(a_ref, b_ref, x_ref, o_ref):
    o_ref[...] = x_ref[...] + _dot(a_ref[...], b_ref[...])


def matmul_residual(a, w, x, tm=512, tn=1024):
    m, k = a.shape
    n = w.shape[1]
    return pl.pallas_call(
        _mm_res_body,
        grid=(n // tn, m // tm),
        in_specs=[pl.BlockSpec((tm, k), lambda j, i: (i, 0)), pl.BlockSpec((k, tn), lambda j, i: (0, j)),
                  pl.BlockSpec((tm, tn), lambda j, i: (i, j))],
        out_specs=pl.BlockSpec((tm, tn), lambda j, i: (i, j)),
        out_shape=jax.ShapeDtypeStruct((m, n), F32),
        compiler_params=_cparams(("parallel", "parallel")),
    )(a, w, x)


def _merge_body(ya_ref, yb_ref, yc_ref, yd_ref, wa_ref, wb_ref, wc_ref, wd_ref,
                ga_ref, gb_ref, gc_ref, gd_ref, o_ref):
    def gate(g_ref):
        return jax.nn.sigmoid(g_ref[...].astype(F32))

    acc = gate(ga_ref) * _dot(ya_ref[...], wa_ref[...])
    acc = acc + gate(gb_ref) * _dot(yb_ref[...], wb_ref[...])
    acc = acc + gate(gc_ref) * _dot(yc_ref[...], wc_ref[...])
    acc = acc + gate(gd_ref) * _dot(yd_ref[...], wd_ref[...])
    o_ref[...] = acc.astype(o_ref.dtype)


def gated_merge(ys, w_branch, proj, tm=512, tn=1024):
    m, kb = ys[0].shape
    d = w_branch.shape[2]
    gpb = d // tn
    g0 = CB_GATE0 * BRANCH_W // tn
    y_specs = [pl.BlockSpec((tm, kb), lambda j, i: (i, 0)) for _ in range(N_BRANCH)]
    w_specs = [pl.BlockSpec((None, kb, tn), functools.partial(lambda j, i, br: (br, 0, j), br=br))
               for br in range(N_BRANCH)]
    g_specs = [pl.BlockSpec((tm, tn), functools.partial(lambda j, i, br: (i, g0 + br * gpb + j), br=br))
               for br in range(N_BRANCH)]
    return pl.pallas_call(
        _merge_body,
        grid=(d // tn, m // tm),
        in_specs=y_specs + w_specs + g_specs,
        out_specs=pl.BlockSpec((tm, tn), lambda j, i: (i, j)),
        out_shape=jax.ShapeDtypeStruct((m, d), BF16),
        compiler_params=_cparams(("parallel", "parallel")),
    )(*ys, w_branch, w_branch, w_branch, w_branch, proj, proj, proj, proj)


def _sb_body(q_ref, k_ref, v_ref, o_ref, *, tq, tk, scale):
    qi = pl.program_id(2)
    r = tq // tk
    q = q_ref[...]
    ur = lax.broadcasted_iota(jnp.int32, (tk, 2 * tk), 0)
    uc = lax.broadcasted_iota(jnp.int32, (tk, 2 * tk), 1)
    cum_w = jnp.where((uc >= tk) | (ur > uc), 1.0, 0.0).astype(BF16)

    def key_block(start, acc, c, mask):
        k = k_ref[pl.ds(start, tq), :]
        v = v_ref[pl.ds(start, tq), :]
        z = _dot_nt(q, k) * scale
        nz = -z
        lg = jnp.log(1.0 + jnp.exp(jnp.minimum(z, nz)))
        l1mb = jnp.minimum(nz, 0.0) - lg
        log_beta = l1mb + z
        if mask is not None:
            l1mb = jnp.where(mask, l1mb, 0.0)
        l1mb_b = l1mb.astype(BF16)
        ws = [None] * r
        for j in range(r - 1, -1, -1):
            sl = slice(j * tk, (j + 1) * tk)
            t = _dot(l1mb_b[:, sl], cum_w)
            w = jnp.exp(log_beta[:, sl] + t[:, :tk] + c)
            if mask is not None:
                w = jnp.where(mask[:, sl], w, 0.0)
            ws[j] = w.astype(BF16)
            c = c + t[:, tk:]
        acc = acc + _dot(jnp.concatenate(ws, axis=1), v)
        return acc, c

    acc = jnp.zeros((tq, HEAD_DIM), F32)
    c = jnp.zeros((tq, tk), F32)
    rows = lax.broadcasted_iota(jnp.int32, (tq, tq), 0)
    cols = lax.broadcasted_iota(jnp.int32, (tq, tq), 1)
    acc, c = key_block(pl.multiple_of(qi * tq, tq), acc, c, cols < rows)

    def body(it, carry):
        return key_block(pl.multiple_of((qi - 1 - it) * tq, tq), *carry, None)

    acc, c = lax.fori_loop(0, qi, body, (acc, c))
    o_ref[...] = acc.astype(o_ref.dtype)


def sb_attention(proj, batch, seq, tq=512, tk=128):
    m = proj.shape[0]
    nq = seq // tq
    return pl.pallas_call(
        functools.partial(_sb_body, tq=tq, tk=tk, scale=HEAD_DIM ** -0.5),
        grid=(batch, N_HEADS, nq),
        in_specs=[pl.BlockSpec((tq, HEAD_DIM), lambda b, h, i: (b * nq + i, CB_SB_Q * N_HEADS + h)),
                  pl.BlockSpec((seq, HEAD_DIM), lambda b, h, i: (b, CB_SB_K * N_HEADS + h)),
                  pl.BlockSpec((seq, HEAD_DIM), lambda b, h, i: (b, CB_SB_V * N_HEADS + h))],
        out_specs=pl.BlockSpec((tq, HEAD_DIM), lambda b, h, i: (b * nq + i, h)),
        out_shape=jax.ShapeDtypeStruct((m, BRANCH_W), BF16),
        compiler_params=_cparams(("parallel", "parallel", "arbitrary")),
    )(proj, proj, proj)


def _mlstm_body(q_ref, k_ref, v_ref, o_ref, g_ref, cw_ref, cb_ref, gb_ref, hn_ref, y_ref,
                xs_ref, c_ref, n_ref, m_ref, *, n_conv):
    L = CHUNK
    chunk = pl.program_id(1)

    @pl.when(chunk == 0)
    def _():
        xs_ref[0:8, :] = jnp.zeros((8, 2 * BRANCH_W), F32)
        c_ref[...] = jnp.zeros_like(c_ref)
        n_ref[...] = jnp.zeros_like(n_ref)
        m_ref[...] = jnp.zeros_like(m_ref)

    xs_ref[8:8 + L, 0:BRANCH_W] = q_ref[...].astype(F32)
    xs_ref[8:8 + L, BRANCH_W:2 * BRANCH_W] = k_ref[...].astype(F32)
    conv = jnp.zeros((L, 2 * BRANCH_W), F32) + cb_ref[...]
    for j in range(n_conv):
        conv = conv + cw_ref[j:j + 1, :] * xs_ref[pl.ds(8 - (n_conv - 1) + j, L), :]
    xs_ref[0:8, :] = xs_ref[L:L + 8, :]
    qk = conv * jax.nn.sigmoid(conv)
    qf = qk[:, 0:BRANCH_W]
    kf = qk[:, BRANCH_W:2 * BRANCH_W] * (HEAD_DIM ** -0.5)

    g = g_ref[...] + gb_ref[...]
    lf = -_softplus(-g)
    ri = lax.broadcasted_iota(jnp.int32, (L, L), 0)
    ci = lax.broadcasted_iota(jnp.int32, (L, L), 1)
    tril = ri >= ci
    tril_f = jnp.where(tril, 1.0, 0.0).astype(F32)
    triu_f = jnp.where(ri <= ci, 1.0, 0.0).astype(F32)
    b_cols = _dot_f32(tril_f, lf)
    g_t = g.T
    b_rows = _dot_f32(lf.T, triu_f)

    for h in range(N_HEADS):
        sl = slice(h * HEAD_DIM, (h + 1) * HEAD_DIM)
        qh = qf[:, sl]
        kh = kf[:, sl]
        qb = qh.astype(BF16)
        kb = kh.astype(BF16)
        vb = v_ref[:, sl]
        i_col = _lane_col(g, h)
        b_col = _lane_col(b_cols, N_HEADS + h)
        i_row = g_t[h:h + 1, :]
        b_row = b_rows[N_HEADS + h:N_HEADS + h + 1, :]
        m_prev = m_ref[h:h + 1, 0:1]
        c_prev = c_ref[h]
        n_prev = n_ref[h:h + 1, :]

        log_d = jnp.where(tril, b_col - b_row + i_row, -jnp.inf)
        m_inter = b_col + m_prev
        m_t = jnp.maximum(m_inter, jnp.max(log_d, axis=1, keepdims=True))
        dmat = jnp.exp(log_d - m_t)
        s = _dot_nt(qb, kb) * dmat
        inter = jnp.exp(m_inter - m_t)
        num = _dot(s.astype(BF16), vb) + inter * _dot(qb, c_prev.astype(BF16))
        den = jnp.sum(s, axis=1, keepdims=True) + inter * jnp.sum(qb.astype(F32) * n_prev, axis=1, keepdims=True)
        hh = num / jnp.maximum(jnp.abs(den), jnp.exp(-m_t))

        m_new = m_t[L - 1:L, :]
        b_last = b_col[L - 1:L, :]
        w_col = jnp.exp(b_last - b_col + i_col - m_new)
        decay = jnp.exp(b_last + m_prev - m_new)
        kw = kh * w_col
        c_ref[h] = decay * c_prev + _dot_tn(kw.astype(BF16), vb)
        n_ref[h:h + 1, :] = decay * n_prev + jnp.sum(kw, axis=0, keepdims=True)
        m_ref[h:h + 1, :] = jnp.broadcast_to(m_new, (1, LANES))

        hn = hh * lax.rsqrt(jnp.mean(hh * hh, axis=1, keepdims=True) + EPS) * hn_ref[:, sl]
        y_ref[:, sl] = (jax.nn.sigmoid(o_ref[:, sl].astype(F32)) * hn).astype(y_ref.dtype)


def mlstm(proj, tail, conv_w, conv_b, gate_b, head_norm, batch, seq):
    m = proj.shape[0]
    nc = seq // CHUNK
    n_conv = conv_w.shape[0]
    gb = jnp.zeros((1, LANES), F32).at[0, 0:2 * N_HEADS].set(gate_b.reshape(-1))

    def blk(cb):
        return pl.BlockSpec((CHUNK, BRANCH_W), lambda b, c: (b * nc + c, cb))

    def const(shape):
        return pl.BlockSpec(shape, lambda b, c: (0, 0))

    return pl.pallas_call(
        functools.partial(_mlstm_body, n_conv=n_conv),
        grid=(batch, nc),
        in_specs=[blk(CB_ML_Q), blk(CB_ML_K), blk(CB_ML_V), blk(CB_ML_O),
                  pl.BlockSpec((CHUNK, LANES), lambda b, c: (b * nc + c, 2)),
                  const((n_conv, 2 * BRANCH_W)), const((1, 2 * BRANCH_W)), const((1, LANES)),
                  const((1, BRANCH_W))],
        out_specs=pl.BlockSpec((CHUNK, BRANCH_W), lambda b, c: (b * nc + c, 0)),
        out_shape=jax.ShapeDtypeStruct((m, BRANCH_W), BF16),
        scratch_shapes=[pltpu.VMEM((CHUNK + 8, 2 * BRANCH_W), F32),
                        pltpu.VMEM((N_HEADS, HEAD_DIM, HEAD_DIM), F32),
                        pltpu.VMEM((N_HEADS, HEAD_DIM), F32),
                        pltpu.VMEM((N_HEADS, LANES), F32)],
        compiler_params=_cparams(("parallel", "arbitrary")),
    )(proj, proj, proj, proj, tail, conv_w, conv_b.reshape(1, -1), gb, head_norm.reshape(1, -1))


def _gelu(x):
    return 0.5 * x * (1.0 + lax.erf(x * (2.0 ** -0.5)))


def _sgu_body(u_ref, v_ref, w_ref, b_ref, o_ref):
    L = CHUNK
    u = _gelu(u_ref[...].astype(F32))
    v = _gelu(v_ref[...].astype(F32))
    mu = jnp.mean(v, axis=1, keepdims=True)
    vc = v - mu
    var = jnp.mean(vc * vc, axis=1, keepdims=True)
    vn = (vc * lax.rsqrt(var + EPS)).astype(BF16)
    ri = lax.broadcasted_iota(jnp.int32, (L, L), 0)
    ci = lax.broadcasted_iota(jnp.int32, (L, L), 1)
    tril = ri >= ci
    bias = b_ref[...]
    for g in range(N_HEADS):
        sl = slice(g * HEAD_DIM, (g + 1) * HEAD_DIM)
        wg = jnp.where(tril, w_ref[g], 0.0).astype(BF16)
        mixed = _dot(wg, vn[:, sl]) + _lane_col(bias, g)
        o_ref[:, sl] = (u[:, sl] * mixed).astype(o_ref.dtype)


def spatial_gating(proj, sg_w, sg_b):
    m = proj.shape[0]
    bias = jnp.zeros((CHUNK, LANES), F32).at[:, 0:N_HEADS].set(sg_b.T)
    return pl.pallas_call(
        _sgu_body,
        grid=(m // CHUNK,),
        in_specs=[pl.BlockSpec((CHUNK, BRANCH_W), lambda i: (i, CB_SG_U)),
                  pl.BlockSpec((CHUNK, BRANCH_W), lambda i: (i, CB_SG_V)),
                  pl.BlockSpec((N_HEADS, CHUNK, CHUNK), lambda i: (0, 0, 0)),
                  pl.BlockSpec((CHUNK, LANES), lambda i: (0, 0))],
        out_specs=pl.BlockSpec((CHUNK, BRANCH_W), lambda i: (i, 0)),
        out_shape=jax.ShapeDtypeStruct((m, BRANCH_W), BF16),
        compiler_params=_cparams(("parallel",)),
    )(proj, proj, sg_w, bias)


def _swa_body(q_ref, kp_ref, kc_ref, vp_ref, vc_ref, qg_ref, kg_ref, sink_ref, o_ref, *, n_chunks):
    W = CHUNK
    n = pl.program_id(0) % n_chunks
    lane = lax.broadcasted_iota(jnp.int32, (1, LANES), 1)
    lo = lane < SW_DIM

    def pair_norm(x, gain):
        x2 = x * x
        s_lo = jnp.sum(jnp.where(lo, x2, 0.0), axis=1, keepdims=True)
        s_hi = jnp.sum(jnp.where(lo, 0.0, x2), axis=1, keepdims=True)
        ms = jnp.where(lo, s_lo, s_hi) * (1.0 / SW_DIM)
        return x * lax.rsqrt(ms + EPS) * gain

    kg = kg_ref[...]
    kk = jnp.concatenate([pair_norm(kp_ref[...], kg), pair_norm(kc_ref[...], kg)], axis=0)
    vv = jnp.concatenate([vp_ref[...], vc_ref[...]], axis=0)
    kk_sw = pltpu.roll(kk, SW_DIM, axis=1)
    vv_sw = pltpu.roll(vv, SW_DIM, axis=1)
    k_ver = [[jnp.where(lo, kk, 0.0).astype(BF16), jnp.where(lo, 0.0, kk_sw).astype(BF16)],
             [jnp.where(lo, kk_sw, 0.0).astype(BF16), jnp.where(lo, 0.0, kk).astype(BF16)]]
    v_ver = [[jnp.where(lo, vv, 0.0).astype(BF16), jnp.where(lo, 0.0, vv_sw).astype(BF16)],
             [jnp.where(lo, vv_sw, 0.0).astype(BF16), jnp.where(lo, 0.0, vv).astype(BF16)]]

    stack = SW_QHEADS // SW_KVHEADS // 2
    qi = lax.broadcasted_iota(jnp.int32, (stack * W, 2 * W), 0) % W
    kj = lax.broadcasted_iota(jnp.int32, (stack * W, 2 * W), 1)
    diff = W + qi - kj
    valid = (diff >= 0) & (diff < W) & ((n > 0) | (kj >= W))
    scale = SW_DIM ** -0.5
    qg = qg_ref[...]
    qn = [pair_norm(q_ref[:, pb * LANES:(pb + 1) * LANES].astype(F32), qg) for pb in range(SW_QHEADS // 2)]
    for hk in range(SW_KVHEADS):
        outs = []
        for half in range(2):
            sel = lo if half == 0 else jnp.logical_not(lo)
            qs = jnp.concatenate([jnp.where(sel, qn[hk * stack + i], 0.0) for i in range(stack)], axis=0)
            s = jnp.where(valid, _dot_nt(qs.astype(BF16), k_ver[hk][half]) * scale, -jnp.inf)
            st = hk * 2 + half
            sink = sink_ref[st * stack * W:(st + 1) * stack * W, :]
            mx = jnp.maximum(jnp.max(s, axis=1, keepdims=True), sink)
            p = jnp.exp(s - mx)
            inv = 1.0 / (jnp.sum(p, axis=1, keepdims=True) + jnp.exp(sink - mx))
            outs.append(_dot(p.astype(BF16), v_ver[hk][half]) * inv)
        for i in range(stack):
            pb = hk * stack + i
            o_ref[:, pb * LANES:(pb + 1) * LANES] = (outs[0][i * W:(i + 1) * W]
                                                     + outs[1][i * W:(i + 1) * W]).astype(o_ref.dtype)


def swa_attention(proj, tail, q_norm, k_norm, sinks, seq):
    m = proj.shape[0]
    nc = seq // CHUNK
    qg = jnp.tile(q_norm.reshape(1, SW_DIM), (1, LANES // SW_DIM))
    kg = jnp.tile(k_norm.reshape(1, SW_DIM), (1, LANES // SW_DIM))
    stack = SW_QHEADS // SW_KVHEADS // 2
    order = [2 * (hk * stack + i) + half for hk in range(SW_KVHEADS) for half in range(2) for i in range(stack)]
    sink_rows = jnp.repeat(sinks.astype(F32)[jnp.array(order)], CHUNK).reshape(SW_QHEADS * CHUNK, 1)

    def prev(i):
        return jnp.maximum(i - 1, 0)

    return pl.pallas_call(
        functools.partial(_swa_body, n_chunks=nc),
        grid=(m // CHUNK,),
        in_specs=[pl.BlockSpec((CHUNK, BRANCH_W), lambda i: (i, CB_SW_Q)),
                  pl.BlockSpec((CHUNK, LANES), lambda i: (prev(i), 0)),
                  pl.BlockSpec((CHUNK, LANES), lambda i: (i, 0)),
                  pl.BlockSpec((CHUNK, LANES), lambda i: (prev(i), 1)),
                  pl.BlockSpec((CHUNK, LANES), lambda i: (i, 1)),
                  pl.BlockSpec((1, LANES), lambda i: (0, 0)),
                  pl.BlockSpec((1, LANES), lambda i: (0, 0)),
                  pl.BlockSpec((SW_QHEADS * CHUNK, 1), lambda i: (0, 0))],
        out_specs=pl.BlockSpec((CHUNK, BRANCH_W), lambda i: (i, 0)),
        out_shape=jax.ShapeDtypeStruct((m, BRANCH_W), BF16),
        compiler_params=_cparams(("parallel",)),
    )(proj, tail, tail, tail, tail, qg, kg, sink_rows)


def _ffn_gu_body(a_ref, bg_ref, bu_ref, o_ref):
    a = a_ref[...]
    g = _dot(a, bg_ref[...])
    u = _dot(a, bu_ref[...])
    o_ref[...] = (g * jax.nn.sigmoid(g) * u).astype(o_ref.dtype)


def ffn_gate_up(h, w_gu, tm=1024, tn=512, sides=()):
    m, k = h.shape
    f = w_gu.shape[1] // 2
    nj = f // tn
    (act,), side_outs = _hosted_call(
        _ffn_gu_body, (nj, m // tm),
        [pl.BlockSpec((tm, k), lambda j, i: (i, 0)),
         pl.BlockSpec((k, tn), lambda j, i: (0, j)),
         pl.BlockSpec((k, tn), lambda j, i: (0, nj + j))],
        [pl.BlockSpec((tm, tn), lambda j, i: (i, j))],
        [jax.ShapeDtypeStruct((m, f), BF16)], (h, w_gu, w_gu), sides)
    return act, side_outs


def _ffn_down_body(a_ref, b_ref, x_ref, o_ref):
    kk = pl.program_id(2)
    part = _dot(a_ref[...], b_ref[...])

    @pl.when(kk == 0)
    def _():
        o_ref[...] = x_ref[...] + part

    @pl.when(kk > 0)
    def _():
        o_ref[...] += part


def ffn_down(act, w_down, x, tm=1024, tn=1024, tk=3584):
    m, f = act.shape
    n = w_down.shape[1]
    nk = f // tk
    return pl.pallas_call(
        _ffn_down_body,
        grid=(n // tn, m // tm, nk),
        in_specs=[pl.BlockSpec((tm, tk), lambda j, i, k: (i, k)),
                  pl.BlockSpec((tk, tn), lambda j, i, k: (k, j)),
                  pl.BlockSpec((tm, tn), lambda j, i, k: (i, j))],
        out_specs=pl.BlockSpec((tm, tn), lambda j, i, k: (i, j)),
        out_shape=jax.ShapeDtypeStruct((m, n), F32),
        compiler_params=_cparams(("parallel", "parallel", "arbitrary")),
    )(act, w_down, x)


MOE_TM = 512
ROW_SLABS = D_MODEL // LANES


def _router_body(x_ref, g_ref, r_ref, h_ref, idx_ref, wts_ref):
    x = x_ref[...]
    ms = jnp.mean(x * x, axis=-1, keepdims=True)
    h = x * lax.rsqrt(ms + EPS) * g_ref[...]
    h_ref[...] = h.astype(h_ref.dtype)
    logits = _dot_f32(h, r_ref[...])
    lane = lax.broadcasted_iota(jnp.int32, logits.shape, 1)
    logits = jnp.where(lane < N_EXPERTS, logits, -jnp.inf)
    m1 = jnp.max(logits, axis=1, keepdims=True)
    i1 = jnp.min(jnp.where(logits == m1, lane, LANES), axis=1, keepdims=True)
    rest = jnp.where(lane == i1, -jnp.inf, logits)
    m2 = jnp.max(rest, axis=1, keepdims=True)
    i2 = jnp.min(jnp.where(rest == m2, lane, LANES), axis=1, keepdims=True)
    e2 = jnp.exp(m2 - m1)
    w1 = 1.0 / (1.0 + e2)
    w2 = e2 / (1.0 + e2)
    idx_ref[...] = jnp.where(lane == 0, i1, jnp.where(lane == 1, i2, 0))
    wts_ref[...] = jnp.where(lane == 0, w1, jnp.where(lane == 1, w2, 0.0))


def norm_and_route(x, g, router, tm=256):
    m, d = x.shape
    r_pad = jnp.zeros((d, LANES), F32).at[:, 0:N_EXPERTS].set(router)
    return pl.pallas_call(
        _router_body,
        grid=(m // tm,),
        in_specs=[pl.BlockSpec((tm, d), lambda i: (i, 0)), pl.BlockSpec((1, d), lambda i: (0, 0)),
                  pl.BlockSpec((d, LANES), lambda i: (0, 0))],
        out_specs=[pl.BlockSpec((tm, d), lambda i: (i, 0)), pl.BlockSpec((tm, LANES), lambda i: (i, 0)),
                   pl.BlockSpec((tm, LANES), lambda i: (i, 0))],
        out_shape=[jax.ShapeDtypeStruct((m, d), BF16), jax.ShapeDtypeStruct((m, LANES), jnp.int32),
                   jax.ShapeDtypeStruct((m, LANES), F32)],
        compiler_params=_cparams(("parallel",)),
    )(x, g.reshape(1, d), r_pad)


def _route_plan(idx, tm):
    m = idx.shape[0]
    n_tiles_max = -(-(2 * m + N_EXPERTS * (tm - 1)) // tm)
    e_flat = jnp.concatenate([idx[:, 0], idx[:, 1]])
    onehot = (e_flat[:, None] == jnp.arange(N_EXPERTS, dtype=jnp.int32)[None, :]).astype(jnp.int32)
    csum = jnp.cumsum(onehot, axis=0)
    counts = csum[-1]
    rank = jnp.sum((csum - onehot) * onehot, axis=1)
    padded = (counts + tm - 1) // tm * tm
    ends = jnp.cumsum(padded)
    starts = ends - padded
    dest = (jnp.sum(onehot * starts[None, :], axis=1) + rank).astype(jnp.int32)
    tile_row0 = jnp.arange(n_tiles_max, dtype=jnp.int32) * tm
    tile_expert = jnp.minimum(jnp.sum((tile_row0[:, None] >= ends[None, :]).astype(jnp.int32), axis=1),
                              N_EXPERTS - 1).astype(jnp.int32)
    n_tiles = (ends[-1] // tm).astype(jnp.int32).reshape(1)
    token = jnp.arange(2 * m, dtype=jnp.int32) % m
    row_token = jnp.zeros((n_tiles_max * tm,), jnp.int32).at[dest].set(token, unique_indices=True)
    return dest, tile_expert, n_tiles, row_token


def _gather_rows_body(idx_ref, src_ref, o_ref, sem, *, tile):
    base = pl.program_id(0) * tile

    def start(r, carry):
        pltpu.make_async_copy(src_ref.at[idx_ref[base + r]], o_ref.at[r], sem).start()
        return carry

    lax.fori_loop(0, tile, start, 0, unroll=8)
    pltpu.make_async_copy(src_ref.at[pl.ds(0, tile)], o_ref, sem).wait()


def gather_rows(src3, idx, tile=1024):
    n = idx.shape[0]
    slab = src3.shape[1:]
    return pl.pallas_call(
        functools.partial(_gather_rows_body, tile=tile),
        grid_spec=pltpu.PrefetchScalarGridSpec(
            num_scalar_prefetch=1, grid=(n // tile,),
            in_specs=[pl.BlockSpec(memory_space=pl.ANY)],
            out_specs=pl.BlockSpec((tile,) + slab, lambda i, idx_ref: (i, 0, 0)),
            scratch_shapes=[pltpu.SemaphoreType.DMA(())]),
        out_shape=jax.ShapeDtypeStruct((n,) + slab, src3.dtype),
        compiler_params=_cparams(("arbitrary",)),
    )(idx, src3)


def _live_tile(t, nt_ref):
    return jnp.minimum(t, nt_ref[0] - 1)


def _moe_gu_body(te_ref, nt_ref, a_ref, bg_ref, bu_ref, o_ref):
    t = pl.program_id(1)

    @pl.when(t < nt_ref[0])
    def _():
        a = a_ref[...]
        g = _dot(a, bg_ref[...])
        u = _dot(a, bu_ref[...])
        o_ref[...] = (g * jax.nn.sigmoid(g) * u).astype(o_ref.dtype)

    @pl.when(t >= nt_ref[0])
    def _():
        o_ref[...] = jnp.zeros_like(o_ref)


def moe_gate_up(xs, w_gu, tile_expert, n_tiles, tm, tn=512):
    rows, k = xs.shape
    f = w_gu.shape[2] // 2
    nj = f // tn
    return pl.pallas_call(
        _moe_gu_body,
        grid_spec=pltpu.PrefetchScalarGridSpec(
            num_scalar_prefetch=2, grid=(nj, rows // tm),
            in_specs=[pl.BlockSpec((tm, k), lambda j, t, te, nt: (_live_tile(t, nt), 0)),
                      pl.BlockSpec((None, k, tn), lambda j, t, te, nt: (te[_live_tile(t, nt)], 0, j)),
                      pl.BlockSpec((None, k, tn), lambda j, t, te, nt: (te[_live_tile(t, nt)], 0, nj + j))],
            out_specs=pl.BlockSpec((tm, tn), lambda j, t, te, nt: (t, j))),
        out_shape=jax.ShapeDtypeStruct((rows, f), BF16),
        compiler_params=_cparams(("arbitrary", "arbitrary")),
    )(tile_expert, n_tiles, xs, w_gu, w_gu)


def _moe_down_body(te_ref, nt_ref, a_ref, b_ref, o_ref):
    t = pl.program_id(1)

    @pl.when(t < nt_ref[0])
    def _():
        o_ref[...] = _dot(a_ref[...], b_ref[...]).astype(o_ref.dtype)

    @pl.when(t >= nt_ref[0])
    def _():
        o_ref[...] = jnp.zeros_like(o_ref)


def moe_down(act, w_down, tile_expert, n_tiles, tm, tn=1024):
    rows, f = act.shape
    n = w_down.shape[2]
    return pl.pallas_call(
        _moe_down_body,
        grid_spec=pltpu.PrefetchScalarGridSpec(
            num_scalar_prefetch=2, grid=(n // tn, rows // tm),
            in_specs=[pl.BlockSpec((tm, f), lambda j, t, te, nt: (_live_tile(t, nt), 0)),
                      pl.BlockSpec((None, f, tn), lambda j, t, te, nt: (te[_live_tile(t, nt)], 0, j))],
            out_specs=pl.BlockSpec((tm, tn), lambda j, t, te, nt: (t, j))),
        out_shape=jax.ShapeDtypeStruct((rows, n), BF16),
        compiler_params=_cparams(("arbitrary", "arbitrary")),
    )(tile_expert, n_tiles, act, w_down)


def _combine_body(x_ref, y1_ref, y2_ref, w_ref, o_ref):
    w = w_ref[...]
    o_ref[...] = (x_ref[...] + _lane_col(w, 0) * y1_ref[...].astype(F32)
                  + _lane_col(w, 1) * y2_ref[...].astype(F32))


def moe_combine(x, yg, wts, tm=256):
    m, d = x.shape
    nb = m // tm
    return pl.pallas_call(
        _combine_body,
        grid=(nb,),
        in_specs=[pl.BlockSpec((tm, d), lambda i: (i, 0)), pl.BlockSpec((tm, d), lambda i: (i, 0)),
                  pl.BlockSpec((tm, d), lambda i: (nb + i, 0)), pl.BlockSpec((tm, LANES), lambda i: (i, 0))],
        out_specs=pl.BlockSpec((tm, d), lambda i: (i, 0)),
        out_shape=jax.ShapeDtypeStruct((m, d), F32),
        compiler_params=_cparams(("parallel",)),
    )(x, yg, yg, wts)


def moe_ffn(x, g, router, w_gu, w_down, tm=MOE_TM):
    m, d = x.shape
    h, idx, wts = norm_and_route(x, g, router)
    dest, tile_expert, n_tiles, row_token = _route_plan(idx, tm)
    rows = row_token.shape[0]
    xs3 = gather_rows(h.reshape(m, ROW_SLABS, d // ROW_SLABS), row_token)
    act = moe_gate_up(xs3.reshape(rows, d), w_gu, tile_expert, n_tiles, tm)
    ys = moe_down(act, w_down, tile_expert, n_tiles, tm)
    yg3 = gather_rows(ys.reshape(rows, ROW_SLABS, d // ROW_SLABS), dest)
    return moe_combine(x, yg3.reshape(2 * m, d), wts)


def _w_in_pieces():
    a_w = BRANCH_W
    o_if = 3 * a_w + 4 * a_w
    o_sg = o_if + 2 * N_HEADS
    o_swk = o_sg + 2 * a_w + SW_QHEADS * SW_DIM
    o_gate = o_swk + 2 * SW_KVHEADS * SW_DIM
    n_in = o_gate + N_BRANCH * D_MODEL
    main = ((0, o_if), (o_sg, o_swk), (o_gate, n_in))
    tail = ((o_swk, o_gate), (o_if, o_sg))
    return main, tail


def _w_in_cast(w_in_all, layer, block_rows=16):
    main_p, tail_p = _w_in_pieces()
    return SideCast(w_in_all, block_rows, ((N_MAIN, main_p), (N_TAIL, tail_p)), layer)


def _branch_casts(w_branch, w_out, layer):
    n_layers, nb, kb, d = w_branch.shape
    return (_plain_cast(w_branch.reshape(n_layers, nb * kb, d), 16, layer), _plain_cast(w_out, 16, layer))


def _mixer(h, w_main, w_tail, layer_params, batch, seq, sides=(), tm=1024):
    (conv_w, conv_b, gate_b, head_norm, sg_w, sg_b, q_norm, k_norm, sinks) = layer_params
    proj, side_outs = inproj_main(h, w_main, tm=tm, sides=sides)
    tail = inproj_tail(h, w_tail)
    y_a = sb_attention(proj, batch, seq)
    y_b = mlstm(proj, tail, conv_w, conv_b, gate_b, head_norm, batch, seq)
    y_c = spatial_gating(proj, sg_w, sg_b)
    y_d = swa_attention(proj, tail, q_norm, k_norm, sinks, seq)
    return proj, (y_a, y_b, y_c, y_d), side_outs


def kernel(x, norm_mix, w_in, ml_conv_w, ml_conv_b, ml_gate_b, ml_head_norm, sg_w, sg_b, sw_q_norm, sw_k_norm,
           sw_sinks, w_branch, w_out, norm_ffn, ffn_w_gu, ffn_w_down, moe_router, moe_w_gu, moe_w_down):
    batch, seq, d = x.shape
    depth = norm_mix.shape[0]
    nb, kb = w_branch.shape[1:3]
    xf = x.reshape(batch * seq, d)
    w_main, w_tail = cast_weights((_w_in_cast(w_in, 0, 64),))
    w_br_b = w_o_b = moe_down_b = None
    for layer in range(depth):
        j = layer // 2
        dense = layer % 2 == 0
        has_next = layer + 1 < depth
        params = (ml_conv_w[layer], ml_conv_b[layer], ml_gate_b[layer], ml_head_norm[layer], sg_w[layer],
                  sg_b[layer], sw_q_norm[layer], sw_k_norm[layer], sw_sinks[layer])
        h = rmsnorm(xf, norm_mix[layer])
        if dense:
            sides = (_plain_cast(ffn_w_gu, 16, j), _plain_cast(ffn_w_down, 64, j)) + _branch_casts(w_branch, w_out, layer)
            proj, ys, (w_gu_b, w_down_b, w_br_b, w_o_b) = _mixer(h, w_main, w_tail, params, batch, seq, sides)
        else:
            n_moe, e, k, f2 = moe_w_gu.shape
            sides = (_plain_cast(moe_w_gu.reshape(n_moe, e * k, f2), 64, j),)
            proj, ys, (moe_gu_b,) = _mixer(h, w_main, w_tail, params, batch, seq, sides, tm=512)
            moe_gu_b = moe_gu_b.reshape(e, k, f2)
            if w_br_b is None:
                w_br_b, w_o_b = cast_weights(_branch_casts(w_branch, w_out, layer))
        merged = gated_merge(ys, w_br_b.reshape(nb, kb, d), proj)
        xf = matmul_residual(merged, w_o_b, xf)
        w_br_b = w_o_b = None
        if dense:
            h = rmsnorm(xf, norm_ffn[layer])
            if has_next:
                n_moe, e, f, n = moe_w_down.shape
                sides = ((_w_in_cast(w_in, layer + 1), _plain_cast(moe_w_down.reshape(n_moe, e * f, n), 64, j))
                         + _branch_casts(w_branch, w_out, layer + 1))
                act, (w_main, w_tail, moe_down_b, w_br_b, w_o_b) = ffn_gate_up(h, w_gu_b, sides=sides)
                moe_down_b = moe_down_b.reshape(e, f, n)
            else:
                act, _ = ffn_gate_up(h, w_gu_b)
            xf = ffn_down(act, w_down_b, xf)
        else:
            if moe_down_b is None:
                moe_down_b = moe_w_down[j].astype(BF16)
            xf = moe_ffn(xf, norm_ffn[layer], moe_router[j], moe_gu_b, moe_down_b)
            moe_down_b = None
            if has_next:
                w_main, w_tail = cast_weights((_w_in_cast(w_in, layer + 1, 64),))
    return xf.reshape(batch, seq, d)
```

```python
import functools
import math
from typing import NamedTuple, Optional

import jax
import jax.numpy as jnp
from jax import lax
from jax.experimental import pallas as pl
from jax.experimental.pallas import tpu as pltpu

F32 = jnp.float32
BF16 = jnp.bfloat16
EPS = 1e-6
LANES = 128
CHUNK = 128
VMEM_LIMIT_MB = 56

D_MODEL = 4096
N_HEADS = 8
HEAD_DIM = 128
BRANCH_W = N_HEADS * HEAD_DIM
SW_QHEADS, SW_KVHEADS, SW_DIM = 16, 2, 64
N_BRANCH = 4
N_EXPERTS = 8

CB_SB_Q, CB_SB_K, CB_SB_V = 0, 1, 2
CB_ML_Q, CB_ML_K, CB_ML_V, CB_ML_O = 3, 4, 5, 6
CB_SG_U, CB_SG_V = 7, 8
CB_SW_Q = 9
CB_GATE0 = 10
N_MAIN = (CB_GATE0 + N_BRANCH * D_MODEL // BRANCH_W) * BRANCH_W
N_TAIL = 3 * LANES


def _cparams(sem, vmem_mb=VMEM_LIMIT_MB):
    return pltpu.CompilerParams(dimension_semantics=sem, vmem_limit_bytes=vmem_mb * 1024 * 1024)


def _dot(a, b):
    return jnp.dot(a, b, preferred_element_type=F32)


def _dot_nt(a, b):
    return lax.dot_general(a, b, (((1,), (1,)), ((), ())), preferred_element_type=F32)


def _dot_tn(a, b):
    return lax.dot_general(a, b, (((0,), (0,)), ((), ())), preferred_element_type=F32)


def _dot_f32(a, b):
    return jnp.dot(a, b, preferred_element_type=F32, precision=lax.Precision.HIGHEST)


def _lane_col(x, idx):
    lane = lax.broadcasted_iota(jnp.int32, x.shape, 1)
    return jnp.sum(jnp.where(lane == idx, x, 0.0), axis=1, keepdims=True)


def _softplus(z):
    return jnp.maximum(z, 0.0) + jnp.log(1.0 + jnp.exp(-jnp.abs(z)))


class SideCast(NamedTuple):
    src: jax.Array
    block_rows: int
    outs: tuple
    lead: Optional[int] = None


def _plain_cast(src, block_rows, lead=None):
    cols = src.shape[-1]
    return SideCast(src, block_rows, ((cols, ((0, cols),)),), lead)


def _hosted_call(body, grid, in_specs, out_specs, out_shape, args, sides):
    n_in, n_out, n_side = len(in_specs), len(out_specs), len(sides)
    steps = math.prod(grid)

    def linear(*ids):
        s = ids[0]
        for g, i in zip(grid[1:], ids[1:]):
            s = s * g + i
        return s

    in_specs, out_specs, out_shape, args = list(in_specs), list(out_specs), list(out_shape), list(args)
    for sc in sides:
        rows, cols = sc.src.shape[-2:]
        br = sc.block_rows
        while rows % br or rows // br > steps:
            br += 16
        nblk = rows // br

        def imap(*ids, nblk=nblk):
            return (jnp.minimum(linear(*ids), nblk - 1), 0)

        if sc.lead is None:
            in_specs.append(pl.BlockSpec((br, cols), imap))
        else:
            in_specs.append(pl.BlockSpec((None, br, cols), lambda *ids, imap=imap, lead=sc.lead: (lead,) + imap(*ids)))
        args.append(sc.src)
        for width, _ in sc.outs:
            out_specs.append(pl.BlockSpec((br, width), imap))
            out_shape.append(jax.ShapeDtypeStruct((rows, width), BF16))

    def hosted(*refs):
        outs = refs[n_in + n_side:]
        body(*refs[:n_in], *outs[:n_out])
        k = n_out
        for s_ref, sc in zip(refs[n_in:n_in + n_side], sides):
            v = s_ref[...]
            for width, pieces in sc.outs:
                dst = outs[k]
                k += 1
                off = 0
                for lo, hi in pieces:
                    dst[:, off:off + hi - lo] = v[:, lo:hi].astype(BF16)
                    off += hi - lo
                if off < width:
                    dst[:, off:width] = jnp.zeros((v.shape[0], width - off), BF16)

    res = pl.pallas_call(
        hosted, grid=grid, in_specs=in_specs, out_specs=out_specs, out_shape=out_shape,
        compiler_params=_cparams(("arbitrary",) * len(grid)),
    )(*args)
    return res[:n_out], res[n_out:]


def cast_weights(sides):
    steps = max(sc.src.shape[-2] // sc.block_rows for sc in sides)
    return _hosted_call(lambda: None, (steps,), [], [], [], (), sides)[1]


def _rmsnorm_body(x_ref, g_ref, o_ref):
    x = x_ref[...]
    ms = jnp.mean(x * x, axis=-1, keepdims=True)
    o_ref[...] = (x * lax.rsqrt(ms + EPS) * g_ref[...]).astype(o_ref.dtype)


def rmsnorm(x, g, tm=256):
    m, d = x.shape
    return pl.pallas_call(
        _rmsnorm_body,
        grid=(m // tm,),
        in_specs=[pl.BlockSpec((tm, d), lambda i: (i, 0)), pl.BlockSpec((1, d), lambda i: (0, 0))],
        out_specs=pl.BlockSpec((tm, d), lambda i: (i, 0)),
        out_shape=jax.ShapeDtypeStruct((m, d), BF16),
        compiler_params=_cparams(("parallel",)),
    )(x, g.reshape(1, d))


def _mm_body(a_ref, b_ref, o_ref):
    o_ref[...] = _dot(a_ref[...], b_ref[...]).astype(o_ref.dtype)


def inproj_main(h, w, tm=1024, tn=1024, sides=()):
    m, k = h.shape
    n = w.shape[1]
    (proj,), side_outs = _hosted_call(
        _mm_body, (n // tn, m // tm),
        [pl.BlockSpec((tm, k), lambda j, i: (i, 0)), pl.BlockSpec((k, tn), lambda j, i: (0, j))],
        [pl.BlockSpec((tm, tn), lambda j, i: (i, j))],
        [jax.ShapeDtypeStruct((m, n), BF16)], (h, w), sides)
    return proj, side_outs


def inproj_tail(h, w, tm=1024):
    m, k = h.shape
    n = w.shape[1]
    return pl.pallas_call(
        _mm_body,
        grid=(m // tm,),
        in_specs=[pl.BlockSpec((tm, k), lambda i: (i, 0)), pl.BlockSpec((k, n), lambda i: (0, 0))],
        out_specs=pl.BlockSpec((tm, n), lambda i: (i, 0)),
        out_shape=jax.ShapeDtypeStruct((m, n), F32),
        compiler_params=_cparams(("parallel",)),
    )(h, w)


def _mm_res_body(a_ref, b_ref, x_ref, o_ref):
    o_ref[...] = x_ref[...] + _dot(a_ref[...], b_ref[...])


def matmul_residual(a, w, x, tm=512, tn=1024):
    m, k = a.shape
    n = w.shape[1]
    return pl.pallas_call(
        _mm_res_body,
        grid=(n // tn, m // tm),
        in_specs=[pl.BlockSpec((tm, k), lambda j, i: (i, 0)), pl.BlockSpec((k, tn), lambda j, i: (0, j)),
                  pl.BlockSpec((tm, tn), lambda j, i: (i, j))],
        out_specs=pl.BlockSpec((tm, tn), lambda j, i: (i, j)),
        out_shape=jax.ShapeDtypeStruct((m, n), F32),
        compiler_params=_cparams(("parallel", "parallel")),
    )(a, w, x)


def _merge_body(ya_ref, yb_ref, yc_ref, yd_ref, wa_ref, wb_ref, wc_ref, wd_ref,
                ga_ref, gb_ref, gc_ref, gd_ref, o_ref):
    def gate(g_ref):
        return jax.nn.sigmoid(g_ref[...].astype(F32))

    acc = gate(ga_ref) * _dot(ya_ref[...], wa_ref[...])
    acc = acc + gate(gb_ref) * _dot(yb_ref[...], wb_ref[...])
    acc = acc + gate(gc_ref) * _dot(yc_ref[...], wc_ref[...])
    acc = acc + gate(gd_ref) * _dot(yd_ref[...], wd_ref[...])
    o_ref[...] = acc.astype(o_ref.dtype)


def gated_merge(ys, w_branch, proj, tm=512, tn=1024):
    m, kb = ys[0].shape
    d = w_branch.shape[2]
    gpb = d // tn
    g0 = CB_GATE0 * BRANCH_W // tn
    y_specs = [pl.BlockSpec((tm, kb), lambda j, i: (i, 0)) for _ in range(N_BRANCH)]
    w_specs = [pl.BlockSpec((None, kb, tn), functools.partial(lambda j, i, br: (br, 0, j), br=br))
               for br in range(N_BRANCH)]
    g_specs = [pl.BlockSpec((tm, tn), functools.partial(lambda j, i, br: (i, g0 + br * gpb + j), br=br))
               for br in range(N_BRANCH)]
    return pl.pallas_call(
        _merge_body,
        grid=(d // tn, m // tm),
        in_specs=y_specs + w_specs + g_specs,
        out_specs=pl.BlockSpec((tm, tn), lambda j, i: (i, j)),
        out_shape=jax.ShapeDtypeStruct((m, d), BF16),
        compiler_params=_cparams(("parallel", "parallel")),
    )(*ys, w_branch, w_branch, w_branch, w_branch, proj, proj, proj, proj)


def _sb_body(q_ref, k_ref, v_ref, o_ref, *, tq, tk, scale):
    qi = pl.program_id(2)
    r = tq // tk
    q = q_ref[...]
    ur = lax.broadcasted_iota(jnp.int32, (tk, 2 * tk), 0)
    uc = lax.broadcasted_iota(jnp.int32, (tk, 2 * tk), 1)
    cum_w = jnp.where((uc >= tk) | (ur > uc), 1.0, 0.0).astype(BF16)

    def key_block(start, acc, c, mask):
        k = k_ref[pl.ds(start, tq), :]
        v = v_ref[pl.ds(start, tq), :]
        z = _dot_nt(q, k) * scale
        nz = -z
        lg = jnp.log(1.0 + jnp.exp(jnp.minimum(z, nz)))
        l1mb = jnp.minimum(nz, 0.0) - lg
        log_beta = l1mb + z
        if mask is not None:
            l1mb = jnp.where(mask, l1mb, 0.0)
        l1mb_b = l1mb.astype(BF16)
        ws = [None] * r
        for j in range(r - 1, -1, -1):
            sl = slice(j * tk, (j + 1) * tk)
            t = _dot(l1mb_b[:, sl], cum_w)
            w = jnp.exp(log_beta[:, sl] + t[:, :tk] + c)
            if mask is not None:
                w = jnp.where(mask[:, sl], w, 0.0)
            ws[j] = w.astype(BF16)
            c = c + t[:, tk:]
        acc = acc + _dot(jnp.concatenate(ws, axis=1), v)
        return acc, c

    acc = jnp.zeros((tq, HEAD_DIM), F32)
    c = jnp.zeros((tq, tk), F32)
    rows = lax.broadcasted_iota(jnp.int32, (tq, tq), 0)
    cols = lax.broadcasted_iota(jnp.int32, (tq, tq), 1)
    acc, c = key_block(pl.multiple_of(qi * tq, tq), acc, c, cols < rows)

    def body(it, carry):
        return key_block(pl.multiple_of((qi - 1 - it) * tq, tq), *carry, None)

    acc, c = lax.fori_loop(0, qi, body, (acc, c))
    o_ref[...] = acc.astype(o_ref.dtype)


def sb_attention(proj, batch, seq, tq=512, tk=128, sides=()):
    m = proj.shape[0]
    nq = seq // tq
    (y,), side_outs = _hosted_call(
        functools.partial(_sb_body, tq=tq, tk=tk, scale=HEAD_DIM ** -0.5), (batch, N_HEADS, nq),
        [pl.BlockSpec((tq, HEAD_DIM), lambda b, h, i: (b * nq + i, CB_SB_Q * N_HEADS + h)),
         pl.BlockSpec((seq, HEAD_DIM), lambda b, h, i: (b, CB_SB_K * N_HEADS + h)),
         pl.BlockSpec((seq, HEAD_DIM), lambda b, h, i: (b, CB_SB_V * N_HEADS + h))],
        [pl.BlockSpec((tq, HEAD_DIM), lambda b, h, i: (b * nq + i, h))],
        [jax.ShapeDtypeStruct((m, BRANCH_W), BF16)], (proj, proj, proj), sides)
    return y, side_outs


def _mlstm_body(q_ref, k_ref, v_ref, o_ref, g_ref, cw_ref, cb_ref, gb_ref, hn_ref, y_ref,
                xs_ref, c_ref, n_ref, m_ref, *, n_conv):
    L = CHUNK
    chunk = pl.program_id(1)

    @pl.when(chunk == 0)
    def _():
        xs_ref[0:8, :] = jnp.zeros((8, 2 * BRANCH_W), F32)
        c_ref[...] = jnp.zeros_like(c_ref)
        n_ref[...] = jnp.zeros_like(n_ref)
        m_ref[...] = jnp.zeros_like(m_ref)

    xs_ref[8:8 + L, 0:BRANCH_W] = q_ref[...].astype(F32)
    xs_ref[8:8 + L, BRANCH_W:2 * BRANCH_W] = k_ref[...].astype(F32)
    conv = jnp.zeros((L, 2 * BRANCH_W), F32) + cb_ref[...]
    for j in range(n_conv):
        conv = conv + cw_ref[j:j + 1, :] * xs_ref[pl.ds(8 - (n_conv - 1) + j, L), :]
    xs_ref[0:8, :] = xs_ref[L:L + 8, :]
    qk = conv * jax.nn.sigmoid(conv)
    qf = qk[:, 0:BRANCH_W]
    kf = qk[:, BRANCH_W:2 * BRANCH_W] * (HEAD_DIM ** -0.5)

    g = g_ref[...] + gb_ref[...]
    lf = -_softplus(-g)
    ri = lax.broadcasted_iota(jnp.int32, (L, L), 0)
    ci = lax.broadcasted_iota(jnp.int32, (L, L), 1)
    tril = ri >= ci
    tril_f = jnp.where(tril, 1.0, 0.0).astype(F32)
    triu_f = jnp.where(ri <= ci, 1.0, 0.0).astype(F32)
    b_cols = _dot_f32(tril_f, lf)
    g_t = g.T
    b_rows = _dot_f32(lf.T, triu_f)

    for h in range(N_HEADS):
        sl = slice(h * HEAD_DIM, (h + 1) * HEAD_DIM)
        qh = qf[:, sl]
        kh = kf[:, sl]
        qb = qh.astype(BF16)
        kb = kh.astype(BF16)
        vb = v_ref[:, sl]
        i_col = _lane_col(g, h)
        b_col = _lane_col(b_cols, N_HEADS + h)
        i_row = g_t[h:h + 1, :]
        b_row = b_rows[N_HEADS + h:N_HEADS + h + 1, :]
        m_prev = m_ref[h:h + 1, 0:1]
        c_prev = c_ref[h]
        n_prev = n_ref[h:h + 1, :]

        log_d = jnp.where(tril, b_col - b_row + i_row, -jnp.inf)
        m_inter = b_col + m_prev
        m_t = jnp.maximum(m_inter, jnp.max(log_d, axis=1, keepdims=True))
        dmat = jnp.exp(log_d - m_t)
        s = _dot_nt(qb, kb) * dmat
        inter = jnp.exp(m_inter - m_t)
        num = _dot(s.astype(BF16), vb) + inter * _dot(qb, c_prev.astype(BF16))
        den = jnp.sum(s, axis=1, keepdims=True) + inter * jnp.sum(qb.astype(F32) * n_prev, axis=1, keepdims=True)
        hh = num / jnp.maximum(jnp.abs(den), jnp.exp(-m_t))

        m_new = m_t[L - 1:L, :]
        b_last = b_col[L - 1:L, :]
        w_col = jnp.exp(b_last - b_col + i_col - m_new)
        decay = jnp.exp(b_last + m_prev - m_new)
        kw = kh * w_col
        c_ref[h] = decay * c_prev + _dot_tn(kw.astype(BF16), vb)
        n_ref[h:h + 1, :] = decay * n_prev + jnp.sum(kw, axis=0, keepdims=True)
        m_ref[h:h + 1, :] = jnp.broadcast_to(m_new, (1, LANES))

        hn = hh * lax.rsqrt(jnp.mean(hh * hh, axis=1, keepdims=True) + EPS) * hn_ref[:, sl]
        y_ref[:, sl] = (jax.nn.sigmoid(o_ref[:, sl].astype(F32)) * hn).astype(y_ref.dtype)


def mlstm(proj, tail, conv_w, conv_b, gate_b, head_norm, batch, seq):
    m = proj.shape[0]
    nc = seq // CHUNK
    n_conv = conv_w.shape[0]
    gb = jnp.zeros((1, LANES), F32).at[0, 0:2 * N_HEADS].set(gate_b.reshape(-1))

    def blk(cb):
        return pl.BlockSpec((CHUNK, BRANCH_W), lambda b, c: (b * nc + c, cb))

    def const(shape):
        return pl.BlockSpec(shape, lambda b, c: (0, 0))

    return pl.pallas_call(
        functools.partial(_mlstm_body, n_conv=n_conv),
        grid=(batch, nc),
        in_specs=[blk(CB_ML_Q), blk(CB_ML_K), blk(CB_ML_V), blk(CB_ML_O),
                  pl.BlockSpec((CHUNK, LANES), lambda b, c: (b * nc + c, 2)),
                  const((n_conv, 2 * BRANCH_W)), const((1, 2 * BRANCH_W)), const((1, LANES)),
                  const((1, BRANCH_W))],
        out_specs=pl.BlockSpec((CHUNK, BRANCH_W), lambda b, c: (b * nc + c, 0)),
        out_shape=jax.ShapeDtypeStruct((m, BRANCH_W), BF16),
        scratch_shapes=[pltpu.VMEM((CHUNK + 8, 2 * BRANCH_W), F32),
                        pltpu.VMEM((N_HEADS, HEAD_DIM, HEAD_DIM), F32),
                        pltpu.VMEM((N_HEADS, HEAD_DIM), F32),
                        pltpu.VMEM((N_HEADS, LANES), F32)],
        compiler_params=_cparams(("parallel", "arbitrary")),
    )(proj, proj, proj, proj, tail, conv_w, conv_b.reshape(1, -1), gb, head_norm.reshape(1, -1))


def _gelu(x):
    return 0.5 * x * (1.0 + lax.erf(x * (2.0 ** -0.5)))


def _sgu_body(u_ref, v_ref, w_ref, b_ref, o_ref):
    L = CHUNK
    u = _gelu(u_ref[...].astype(F32))
    v = _gelu(v_ref[...].astype(F32))
    mu = jnp.mean(v, axis=1, keepdims=True)
    vc = v - mu
    var = jnp.mean(vc * vc, axis=1, keepdims=True)
    vn = (vc * lax.rsqrt(var + EPS)).astype(BF16)
    ri = lax.broadcasted_iota(jnp.int32, (L, L), 0)
    ci = lax.broadcasted_iota(jnp.int32, (L, L), 1)
    tril = ri >= ci
    bias = b_ref[...]
    for g in range(N_HEADS):
        sl = slice(g * HEAD_DIM, (g + 1) * HEAD_DIM)
        wg = jnp.where(tril, w_ref[g], 0.0).astype(BF16)
        mixed = _dot(wg, vn[:, sl]) + _lane_col(bias, g)
        o_ref[:, sl] = (u[:, sl] * mixed).astype(o_ref.dtype)


def spatial_gating(proj, sg_w, sg_b):
    m = proj.shape[0]
    bias = jnp.zeros((CHUNK, LANES), F32).at[:, 0:N_HEADS].set(sg_b.T)
    return pl.pallas_call(
        _sgu_body,
        grid=(m // CHUNK,),
        in_specs=[pl.BlockSpec((CHUNK, BRANCH_W), lambda i: (i, CB_SG_U)),
                  pl.BlockSpec((CHUNK, BRANCH_W), lambda i: (i, CB_SG_V)),
                  pl.BlockSpec((N_HEADS, CHUNK, CHUNK), lambda i: (0, 0, 0)),
                  pl.BlockSpec((CHUNK, LANES), lambda i: (0, 0))],
        out_specs=pl.BlockSpec((CHUNK, BRANCH_W), lambda i: (i, 0)),
        out_shape=jax.ShapeDtypeStruct((m, BRANCH_W), BF16),
        compiler_params=_cparams(("parallel",)),
    )(proj, proj, sg_w, bias)


def _swa_body(q_ref, kp_ref, kc_ref, vp_ref, vc_ref, qg_ref, kg_ref, sink_ref, o_ref, *, n_chunks):
    W = CHUNK
    n = pl.program_id(0) % n_chunks
    lane = lax.broadcasted_iota(jnp.int32, (1, LANES), 1)
    lo = lane < SW_DIM

    def pair_norm(x, gain):
        x2 = x * x
        s_lo = jnp.sum(jnp.where(lo, x2, 0.0), axis=1, keepdims=True)
        s_hi = jnp.sum(jnp.where(lo, 0.0, x2), axis=1, keepdims=True)
        ms = jnp.where(lo, s_lo, s_hi) * (1.0 / SW_DIM)
        return x * lax.rsqrt(ms + EPS) * gain

    kg = kg_ref[...]
    kk = jnp.concatenate([pair_norm(kp_ref[...], kg), pair_norm(kc_ref[...], kg)], axis=0)
    vv = jnp.concatenate([vp_ref[...], vc_ref[...]], axis=0)
    kk_sw = pltpu.roll(kk, SW_DIM, axis=1)
    vv_sw = pltpu.roll(vv, SW_DIM, axis=1)
    k_ver = [[jnp.where(lo, kk, 0.0).astype(BF16), jnp.where(lo, 0.0, kk_sw).astype(BF16)],
             [jnp.where(lo, kk_sw, 0.0).astype(BF16), jnp.where(lo, 0.0, kk).astype(BF16)]]
    v_ver = [[jnp.where(lo, vv, 0.0).astype(BF16), jnp.where(lo, 0.0, vv_sw).astype(BF16)],
             [jnp.where(lo, vv_sw, 0.0).astype(BF16), jnp.where(lo, 0.0, vv).astype(BF16)]]

    stack = SW_QHEADS // SW_KVHEADS // 2
    qi = lax.broadcasted_iota(jnp.int32, (stack * W, 2 * W), 0) % W
    kj = lax.broadcasted_iota(jnp.int32, (stack * W, 2 * W), 1)
    diff = W + qi - kj
    valid = (diff >= 0) & (diff < W) & ((n > 0) | (kj >= W))
    scale = SW_DIM ** -0.5
    qg = qg_ref[...]
    qn = [pair_norm(q_ref[:, pb * LANES:(pb + 1) * LANES].astype(F32), qg) for pb in range(SW_QHEADS // 2)]
    for hk in range(SW_KVHEADS):
        outs = []
        for half in range(2):
            sel = lo if half == 0 else jnp.logical_not(lo)
            qs = jnp.concatenate([jnp.where(sel, qn[hk * stack + i], 0.0) for i in range(stack)], axis=0)
            s = jnp.where(valid, _dot_nt(qs.astype(BF16), k_ver[hk][half]) * scale, -jnp.inf)
            st = hk * 2 + half
            sink = sink_ref[st * stack * W:(st + 1) * stack * W, :]
            mx = jnp.maximum(jnp.max(s, axis=1, keepdims=True), sink)
            p = jnp.exp(s - mx)
            inv = 1.0 / (jnp.sum(p, axis=1, keepdims=True) + jnp.exp(sink - mx))
            outs.append(_dot(p.astype(BF16), v_ver[hk][half]) * inv)
        for i in range(stack):
            pb = hk * stack + i
            o_ref[:, pb * LANES:(pb + 1) * LANES] = (outs[0][i * W:(i + 1) * W]
                                                     + outs[1][i * W:(i + 1) * W]).astype(o_ref.dtype)


def swa_attention(proj, tail, q_norm, k_norm, sinks, seq):
    m = proj.shape[0]
    nc = seq // CHUNK
    qg = jnp.tile(q_norm.reshape(1, SW_DIM), (1, LANES // SW_DIM))
    kg = jnp.tile(k_norm.reshape(1, SW_DIM), (1, LANES // SW_DIM))
    stack = SW_QHEADS // SW_KVHEADS // 2
    order = [2 * (hk * stack + i) + half for hk in range(SW_KVHEADS) for half in range(2) for i in range(stack)]
    sink_rows = jnp.repeat(sinks.astype(F32)[jnp.array(order)], CHUNK).reshape(SW_QHEADS * CHUNK, 1)

    def prev(i):
        return jnp.maximum(i - 1, 0)

    return pl.pallas_call(
        functools.partial(_swa_body, n_chunks=nc),
        grid=(m // CHUNK,),
        in_specs=[pl.BlockSpec((CHUNK, BRANCH_W), lambda i: (i, CB_SW_Q)),
                  pl.BlockSpec((CHUNK, LANES), lambda i: (prev(i), 0)),
                  pl.BlockSpec((CHUNK, LANES), lambda i: (i, 0)),
                  pl.BlockSpec((CHUNK, LANES), lambda i: (prev(i), 1)),
                  pl.BlockSpec((CHUNK, LANES), lambda i: (i, 1)),
                  pl.BlockSpec((1, LANES), lambda i: (0, 0)),
                  pl.BlockSpec((1, LANES), lambda i: (0, 0)),
                  pl.BlockSpec((SW_QHEADS * CHUNK, 1), lambda i: (0, 0))],
        out_specs=pl.BlockSpec((CHUNK, BRANCH_W), lambda i: (i, 0)),
        out_shape=jax.ShapeDtypeStruct((m, BRANCH_W), BF16),
        compiler_params=_cparams(("parallel",)),
    )(proj, tail, tail, tail, tail, qg, kg, sink_rows)


def _ffn_gu_body(a_ref, bg_ref, bu_ref, o_ref):
    a = a_ref[...]
    g = _dot(a, bg_ref[...])
    u = _dot(a, bu_ref[...])
    o_ref[...] = (g * jax.nn.sigmoid(g) * u).astype(o_ref.dtype)


def ffn_gate_up(h, w_gu, tm=1024, tn=512, sides=()):
    m, k = h.shape
    f = w_gu.shape[1] // 2
    nj = f // tn
    (act,), side_outs = _hosted_call(
        _ffn_gu_body, (nj, m // tm),
        [pl.BlockSpec((tm, k), lambda j, i: (i, 0)),
         pl.BlockSpec((k, tn), lambda j, i: (0, j)),
         pl.BlockSpec((k, tn), lambda j, i: (0, nj + j))],
        [pl.BlockSpec((tm, tn), lambda j, i: (i, j))],
        [jax.ShapeDtypeStruct((m, f), BF16)], (h, w_gu, w_gu), sides)
    return act, side_outs


def _ffn_down_body(a_ref, b_ref, x_ref, o_ref):
    kk = pl.program_id(2)
    part = _dot(a_ref[...], b_ref[...])

    @pl.when(kk == 0)
    def _():
        o_ref[...] = x_ref[...] + part

    @pl.when(kk > 0)
    def _():
        o_ref[...] += part


def ffn_down(act, w_down, x, tm=1024, tn=1024, tk=3584):
    m, f = act.shape
    n = w_down.shape[1]
    nk = f // tk
    return pl.pallas_call(
        _ffn_down_body,
        grid=(n // tn, m // tm, nk),
        in_specs=[pl.BlockSpec((tm, tk), lambda j, i, k: (i, k)),
                  pl.BlockSpec((tk, tn), lambda j, i, k: (k, j)),
                  pl.BlockSpec((tm, tn), lambda j, i, k: (i, j))],
        out_specs=pl.BlockSpec((tm, tn), lambda j, i, k: (i, j)),
        out_shape=jax.ShapeDtypeStruct((m, n), F32),
        compiler_params=_cparams(("parallel", "parallel", "arbitrary")),
    )(act, w_down, x)


MOE_TM = 512
ROW_SLABS = D_MODEL // LANES


def _router_body(x_ref, g_ref, r_ref, h_ref, idx_ref, wts_ref):
    x = x_ref[...]
    ms = jnp.mean(x * x, axis=-1, keepdims=True)
    h = x * lax.rsqrt(ms + EPS) * g_ref[...]
    h_ref[...] = h.astype(h_ref.dtype)
    logits = _dot_f32(h, r_ref[...])
    lane = lax.broadcasted_iota(jnp.int32, logits.shape, 1)
    logits = jnp.where(lane < N_EXPERTS, logits, -jnp.inf)
    m1 = jnp.max(logits, axis=1, keepdims=True)
    i1 = jnp.min(jnp.where(logits == m1, lane, LANES), axis=1, keepdims=True)
    rest = jnp.where(lane == i1, -jnp.inf, logits)
    m2 = jnp.max(rest, axis=1, keepdims=True)
    i2 = jnp.min(jnp.where(rest == m2, lane, LANES), axis=1, keepdims=True)
    e2 = jnp.exp(m2 - m1)
    w1 = 1.0 / (1.0 + e2)
    w2 = e2 / (1.0 + e2)
    idx_ref[...] = jnp.where(lane == 0, i1, jnp.where(lane == 1, i2, 0))
    wts_ref[...] = jnp.where(lane == 0, w1, jnp.where(lane == 1, w2, 0.0))


def norm_and_route(x, g, router, tm=256):
    m, d = x.shape
    r_pad = jnp.zeros((d, LANES), F32).at[:, 0:N_EXPERTS].set(router)
    return pl.pallas_call(
        _router_body,
        grid=(m // tm,),
        in_specs=[pl.BlockSpec((tm, d), lambda i: (i, 0)), pl.BlockSpec((1, d), lambda i: (0, 0)),
                  pl.BlockSpec((d, LANES), lambda i: (0, 0))],
        out_specs=[pl.BlockSpec((tm, d), lambda i: (i, 0)), pl.BlockSpec((tm, LANES), lambda i: (i, 0)),
                   pl.BlockSpec((tm, LANES), lambda i: (i, 0))],
        out_shape=[jax.ShapeDtypeStruct((m, d), BF16), jax.ShapeDtypeStruct((m, LANES), jnp.int32),
                   jax.ShapeDtypeStruct((m, LANES), F32)],
        compiler_params=_cparams(("parallel",)),
    )(x, g.reshape(1, d), r_pad)


def _route_plan(idx, tm):
    m = idx.shape[0]
    n_tiles_max = -(-(2 * m + N_EXPERTS * (tm - 1)) // tm)
    e_flat = jnp.concatenate([idx[:, 0], idx[:, 1]])
    onehot = (e_flat[:, None] == jnp.arange(N_EXPERTS, dtype=jnp.int32)[None, :]).astype(jnp.int32)
    csum = jnp.cumsum(onehot, axis=0)
    counts = csum[-1]
    rank = jnp.sum((csum - onehot) * onehot, axis=1)
    padded = (counts + tm - 1) // tm * tm
    ends = jnp.cumsum(padded)
    starts = ends - padded
    dest = (jnp.sum(onehot * starts[None, :], axis=1) + rank).astype(jnp.int32)
    tile_row0 = jnp.arange(n_tiles_max, dtype=jnp.int32) * tm
    tile_expert = jnp.minimum(jnp.sum((tile_row0[:, None] >= ends[None, :]).astype(jnp.int32), axis=1),
                              N_EXPERTS - 1).astype(jnp.int32)
    n_tiles = (ends[-1] // tm).astype(jnp.int32).reshape(1)
    token = jnp.arange(2 * m, dtype=jnp.int32) % m
    row_token = jnp.zeros((n_tiles_max * tm,), jnp.int32).at[dest].set(token, unique_indices=True)
    return dest, tile_expert, n_tiles, row_token


def _gather_rows_body(idx_ref, src_ref, o_ref, sem, *, tile):
    base = pl.program_id(0) * tile

    def start(r, carry):
        pltpu.make_async_copy(src_ref.at[idx_ref[base + r]], o_ref.at[r], sem).start()
        return carry

    lax.fori_loop(0, tile, start, 0, unroll=8)
    pltpu.make_async_copy(src_ref.at[pl.ds(0, tile)], o_ref, sem).wait()


def gather_rows(src3, idx, tile=1024):
    n = idx.shape[0]
    slab = src3.shape[1:]
    return pl.pallas_call(
        functools.partial(_gather_rows_body, tile=tile),
        grid_spec=pltpu.PrefetchScalarGridSpec(
            num_scalar_prefetch=1, grid=(n // tile,),
            in_specs=[pl.BlockSpec(memory_space=pl.ANY)],
            out_specs=pl.BlockSpec((tile,) + slab, lambda i, idx_ref: (i, 0, 0)),
            scratch_shapes=[pltpu.SemaphoreType.DMA(())]),
        out_shape=jax.ShapeDtypeStruct((n,) + slab, src3.dtype),
        compiler_params=_cparams(("arbitrary",)),
    )(idx, src3)


def _live_tile(t, nt_ref):
    return jnp.minimum(t, nt_ref[0] - 1)


def _moe_gu_body(te_ref, nt_ref, a_ref, bg_ref, bu_ref, o_ref):
    t = pl.program_id(1)

    @pl.when(t < nt_ref[0])
    def _():
        a = a_ref[...]
        g = _dot(a, bg_ref[...])
        u = _dot(a, bu_ref[...])
        o_ref[...] = (g * jax.nn.sigmoid(g) * u).astype(o_ref.dtype)

    @pl.when(t >= nt_ref[0])
    def _():
        o_ref[...] = jnp.zeros_like(o_ref)


def moe_gate_up(xs, w_gu, tile_expert, n_tiles, tm, tn=512):
    rows, k = xs.shape
    f = w_gu.shape[2] // 2
    nj = f // tn
    return pl.pallas_call(
        _moe_gu_body,
        grid_spec=pltpu.PrefetchScalarGridSpec(
            num_scalar_prefetch=2, grid=(nj, rows // tm),
            in_specs=[pl.BlockSpec((tm, k), lambda j, t, te, nt: (_live_tile(t, nt), 0)),
                      pl.BlockSpec((None, k, tn), lambda j, t, te, nt: (te[_live_tile(t, nt)], 0, j)),
                      pl.BlockSpec((None, k, tn), lambda j, t, te, nt: (te[_live_tile(t, nt)], 0, nj + j))],
            out_specs=pl.BlockSpec((tm, tn), lambda j, t, te, nt: (t, j))),
        out_shape=jax.ShapeDtypeStruct((rows, f), BF16),
        compiler_params=_cparams(("arbitrary", "arbitrary")),
    )(tile_expert, n_tiles, xs, w_gu, w_gu)


def _moe_down_body(te_ref, nt_ref, a_ref, b_ref, o_ref):
    t = pl.program_id(1)

    @pl.when(t < nt_ref[0])
    def _():
        o_ref[...] = _dot(a_ref[...], b_ref[...]).astype(o_ref.dtype)

    @pl.when(t >= nt_ref[0])
    def _():
        o_ref[...] = jnp.zeros_like(o_ref)


def moe_down(act, w_down, tile_expert, n_tiles, tm, tn=1024):
    rows, f = act.shape
    n = w_down.shape[2]
    return pl.pallas_call(
        _moe_down_body,
        grid_spec=pltpu.PrefetchScalarGridSpec(
            num_scalar_prefetch=2, grid=(n // tn, rows // tm),
            in_specs=[pl.BlockSpec((tm, f), lambda j, t, te, nt: (_live_tile(t, nt), 0)),
                      pl.BlockSpec((None, f, tn), lambda j, t, te, nt: (te[_live_tile(t, nt)], 0, j))],
            out_specs=pl.BlockSpec((tm, tn), lambda j, t, te, nt: (t, j))),
        out_shape=jax.ShapeDtypeStruct((rows, n), BF16),
        compiler_params=_cparams(("arbitrary", "arbitrary")),
    )(tile_expert, n_tiles, act, w_down)


def _combine_body(x_ref, y1_ref, y2_ref, w_ref, o_ref):
    w = w_ref[...]
    o_ref[...] = (x_ref[...] + _lane_col(w, 0) * y1_ref[...].astype(F32)
                  + _lane_col(w, 1) * y2_ref[...].astype(F32))


def moe_combine(x, yg, wts, tm=256):
    m, d = x.shape
    nb = m // tm
    return pl.pallas_call(
        _combine_body,
        grid=(nb,),
        in_specs=[pl.BlockSpec((tm, d), lambda i: (i, 0)), pl.BlockSpec((tm, d), lambda i: (i, 0)),
                  pl.BlockSpec((tm, d), lambda i: (nb + i, 0)), pl.BlockSpec((tm, LANES), lambda i: (i, 0))],
        out_specs=pl.BlockSpec((tm, d), lambda i: (i, 0)),
        out_shape=jax.ShapeDtypeStruct((m, d), F32),
        compiler_params=_cparams(("parallel",)),
    )(x, yg, yg, wts)


def moe_ffn(x, g, router, w_gu, w_down, tm=MOE_TM):
    m, d = x.shape
    h, idx, wts = norm_and_route(x, g, router)
    dest, tile_expert, n_tiles, row_token = _route_plan(idx, tm)
    rows = row_token.shape[0]
    xs3 = gather_rows(h.reshape(m, ROW_SLABS, d // ROW_SLABS), row_token)
    act = moe_gate_up(xs3.reshape(rows, d), w_gu, tile_expert, n_tiles, tm)
    ys = moe_down(act, w_down, tile_expert, n_tiles, tm)
    yg3 = gather_rows(ys.reshape(rows, ROW_SLABS, d // ROW_SLABS), dest)
    return moe_combine(x, yg3.reshape(2 * m, d), wts)


def _w_in_pieces():
    a_w = BRANCH_W
    o_if = 3 * a_w + 4 * a_w
    o_sg = o_if + 2 * N_HEADS
    o_swk = o_sg + 2 * a_w + SW_QHEADS * SW_DIM
    o_gate = o_swk + 2 * SW_KVHEADS * SW_DIM
    n_in = o_gate + N_BRANCH * D_MODEL
    main = ((0, o_if), (o_sg, o_swk), (o_gate, n_in))
    tail = ((o_swk, o_gate), (o_if, o_sg))
    return main, tail


def _w_in_cast(w_in_all, layer, block_rows=16):
    main_p, tail_p = _w_in_pieces()
    return SideCast(w_in_all, block_rows, ((N_MAIN, main_p), (N_TAIL, tail_p)), layer)


def _branch_casts(w_branch, w_out, layer):
    n_layers, nb, kb, d = w_branch.shape
    return (_plain_cast(w_branch.reshape(n_layers, nb * kb, d), 16, layer), _plain_cast(w_out, 16, layer))


def _mixer(h, w_main, w_tail, layer_params, batch, seq, sides=()):
    (conv_w, conv_b, gate_b, head_norm, sg_w, sg_b, q_norm, k_norm, sinks) = layer_params
    proj, _ = inproj_main(h, w_main)
    tail = inproj_tail(h, w_tail)
    y_a, side_outs = sb_attention(proj, batch, seq, sides=sides)
    y_b = mlstm(proj, tail, conv_w, conv_b, gate_b, head_norm, batch, seq)
    y_c = spatial_gating(proj, sg_w, sg_b)
    y_d = swa_attention(proj, tail, q_norm, k_norm, sinks, seq)
    return proj, (y_a, y_b, y_c, y_d), side_outs


def kernel(x, norm_mix, w_in, ml_conv_w, ml_conv_b, ml_gate_b, ml_head_norm, sg_w, sg_b, sw_q_norm, sw_k_norm,
           sw_sinks, w_branch, w_out, norm_ffn, ffn_w_gu, ffn_w_down, moe_router, moe_w_gu, moe_w_down):
    batch, seq, d = x.shape
    depth = norm_mix.shape[0]
    nb, kb = w_branch.shape[1:3]
    xf = x.reshape(batch * seq, d)
    w_main, w_tail = cast_weights((_w_in_cast(w_in, 0, 64),))
    w_br_b = w_o_b = moe_down_b = None
    for layer in range(depth):
        j = layer // 2
        dense = layer % 2 == 0
        has_next = layer + 1 < depth
        params = (ml_conv_w[layer], ml_conv_b[layer], ml_gate_b[layer], ml_head_norm[layer], sg_w[layer],
                  sg_b[layer], sw_q_norm[layer], sw_k_norm[layer], sw_sinks[layer])
        h = rmsnorm(xf, norm_mix[layer])
        if dense:
            sides = (_plain_cast(ffn_w_gu, 16, j), _plain_cast(ffn_w_down, 64, j)) + _branch_casts(w_branch, w_out, layer)
            proj, ys, (w_gu_b, w_down_b, w_br_b, w_o_b) = _mixer(h, w_main, w_tail, params, batch, seq, sides)
        else:
            n_moe, e, k, f2 = moe_w_gu.shape
            sides = (_plain_cast(moe_w_gu.reshape(n_moe, e * k, f2), 128, j),)
            proj, ys, (moe_gu_b,) = _mixer(h, w_main, w_tail, params, batch, seq, sides)
            moe_gu_b = moe_gu_b.reshape(e, k, f2)
            if w_br_b is None:
                w_br_b, w_o_b = cast_weights(_branch_casts(w_branch, w_out, layer))
        merged = gated_merge(ys, w_br_b.reshape(nb, kb, d), proj)
        xf = matmul_residual(merged, w_o_b, xf)
        w_br_b = w_o_b = None
        if dense:
            h = rmsnorm(xf, norm_ffn[layer])
            if has_next:
                n_moe, e, f, n = moe_w_down.shape
                sides = ((_w_in_cast(w_in, layer + 1), _plain_cast(moe_w_down.reshape(n_moe, e * f, n), 64, j))
                         + _branch_casts(w_branch, w_out, layer + 1))
                act, (w_main, w_tail, moe_down_b, w_br_b, w_o_b) = ffn_gate_up(h, w_gu_b, sides=sides)
                moe_down_b = moe_down_b.reshape(e, f, n)
            else:
                act, _ = ffn_gate_up(h, w_gu_b)
            xf = ffn_down(act, w_down_b, xf)
        else:
            if moe_down_b is None:
                moe_down_b = moe_w_down[j].astype(BF16)
            xf = moe_ffn(xf, norm_ffn[layer], moe_router[j], moe_gu_b, moe_down_b)
            moe_down_b = None
            if has_next:
                w_main, w_tail = cast_weights((_w_in_cast(w_in, layer + 1, 64),))
    return xf.reshape(batch, seq, d)
```

```python
import functools
import math
from typing import NamedTuple, Optional

import jax
import jax.numpy as jnp
from jax import lax
from jax.experimental import pallas as pl
from jax.experimental.pallas import tpu as pltpu

F32 = jnp.float32
BF16 = jnp.bfloat16
EPS = 1e-6
LANES = 128
CHUNK = 128
VMEM_LIMIT_MB = 56

D_MODEL = 4096
N_HEADS = 8
HEAD_DIM = 128
BRANCH_W = N_HEADS * HEAD_DIM
SW_QHEADS, SW_KVHEADS, SW_DIM = 16, 2, 64
N_BRANCH = 4
N_EXPERTS = 8

CB_SB_Q, CB_SB_K, CB_SB_V = 0, 1, 2
CB_ML_Q, CB_ML_K, CB_ML_V, CB_ML_O = 3, 4, 5, 6
CB_SG_U, CB_SG_V = 7, 8
CB_SW_Q = 9
CB_GATE0 = 10
N_MAIN = (CB_GATE0 + N_BRANCH * D_MODEL // BRANCH_W) * BRANCH_W
N_TAIL = 3 * LANES


def _cparams(sem, vmem_mb=VMEM_LIMIT_MB):
    return pltpu.CompilerParams(dimension_semantics=sem, vmem_limit_bytes=vmem_mb * 1024 * 1024)


def _dot(a, b):
    return jnp.dot(a, b, preferred_element_type=F32)


def _dot_nt(a, b):
    return lax.dot_general(a, b, (((1,), (1,)), ((), ())), preferred_element_type=F32)


def _dot_tn(a, b):
    return lax.dot_general(a, b, (((0,), (0,)), ((), ())), preferred_element_type=F32)


def _dot_f32(a, b):
    return jnp.dot(a, b, preferred_element_type=F32, precision=lax.Precision.HIGHEST)


def _lane_col(x, idx):
    lane = lax.broadcasted_iota(jnp.int32, x.shape, 1)
    return jnp.sum(jnp.where(lane == idx, x, 0.0), axis=1, keepdims=True)


def _softplus(z):
    return jnp.maximum(z, 0.0) + jnp.log(1.0 + jnp.exp(-jnp.abs(z)))


class SideCast(NamedTuple):
    src: jax.Array
    block_rows: int
    outs: tuple
    lead: Optional[int] = None


def _plain_cast(src, block_rows, lead=None):
    cols = src.shape[-1]
    return SideCast(src, block_rows, ((cols, ((0, cols),)),), lead)


def _hosted_call(body, grid, in_specs, out_specs, out_shape, args, sides):
    n_in, n_out, n_side = len(in_specs), len(out_specs), len(sides)
    steps = math.prod(grid)

    def linear(*ids):
        s = ids[0]
        for g, i in zip(grid[1:], ids[1:]):
            s = s * g + i
        return s

    in_specs, out_specs, out_shape, args = list(in_specs), list(out_specs), list(out_shape), list(args)
    for sc in sides:
        rows, cols = sc.src.shape[-2:]
        br = sc.block_rows
        while rows % br or rows // br > steps:
            br += 16
        nblk = rows // br

        def imap(*ids, nblk=nblk):
            return (jnp.minimum(linear(*ids), nblk - 1), 0)

        if sc.lead is None:
            in_specs.append(pl.BlockSpec((br, cols), imap))
        else:
            in_specs.append(pl.BlockSpec((None, br, cols), lambda *ids, imap=imap, lead=sc.lead: (lead,) + imap(*ids)))
        args.append(sc.src)
        for width, _ in sc.outs:
            out_specs.append(pl.BlockSpec((br, width), imap))
            out_shape.append(jax.ShapeDtypeStruct((rows, width), BF16))

    def hosted(*refs):
        outs = refs[n_in + n_side:]
        body(*refs[:n_in], *outs[:n_out])
        k = n_out
        for s_ref, sc in zip(refs[n_in:n_in + n_side], sides):
            v = s_ref[...]
            for width, pieces in sc.outs:
                dst = outs[k]
                k += 1
                off = 0
                for lo, hi in pieces:
                    dst[:, off:off + hi - lo] = v[:, lo:hi].astype(BF16)
                    off += hi - lo
                if off < width:
                    dst[:, off:width] = jnp.zeros((v.shape[0], width - off), BF16)

    res = pl.pallas_call(
        hosted, grid=grid, in_specs=in_specs, out_specs=out_specs, out_shape=out_shape,
        compiler_params=_cparams(("arbitrary",) * len(grid)),
    )(*args)
    return res[:n_out], res[n_out:]


def cast_weights(sides):
    steps = max(sc.src.shape[-2] // sc.block_rows for sc in sides)
    return _hosted_call(lambda: None, (steps,), [], [], [], (), sides)[1]


def _rmsnorm_body(x_ref, g_ref, o_ref):
    x = x_ref[...]
    ms = jnp.mean(x * x, axis=-1, keepdims=True)
    o_ref[...] = (x * lax.rsqrt(ms + EPS) * g_ref[...]).astype(o_ref.dtype)


def rmsnorm(x, g, tm=256):
    m, d = x.shape
    return pl.pallas_call(
        _rmsnorm_body,
        grid=(m // tm,),
        in_specs=[pl.BlockSpec((tm, d), lambda i: (i, 0)), pl.BlockSpec((1, d), lambda i: (0, 0))],
        out_specs=pl.BlockSpec((tm, d), lambda i: (i, 0)),
        out_shape=jax.ShapeDtypeStruct((m, d), BF16),
        compiler_params=_cparams(("parallel",)),
    )(x, g.reshape(1, d))


def _mm_body(a_ref, b_ref, o_ref):
    o_ref[...] = _dot(a_ref[...], b_ref[...]).astype(o_ref.dtype)


def inproj_main(h, w, tm=1024, tn=1024, sides=()):
    m, k = h.shape
    n = w.shape[1]
    (proj,), side_outs = _hosted_call(
        _mm_body, (n // tn, m // tm),
        [pl.BlockSpec((tm, k), lambda j, i: (i, 0)), pl.BlockSpec((k, tn), lambda j, i: (0, j))],
        [pl.BlockSpec((tm, tn), lambda j, i: (i, j))],
        [jax.ShapeDtypeStruct((m, n), BF16)], (h, w), sides)
    return proj, side_outs


def inproj_tail(h, w, tm=1024):
    m, k = h.shape
    n = w.shape[1]
    return pl.pallas_call(
        _mm_body,
        grid=(m // tm,),
        in_specs=[pl.BlockSpec((tm, k), lambda i: (i, 0)), pl.BlockSpec((k, n), lambda i: (0, 0))],
        out_specs=pl.BlockSpec((tm, n), lambda i: (i, 0)),
        out_shape=jax.ShapeDtypeStruct((m, n), F32),
        compiler_params=_cparams(("parallel",)),
    )(h, w)


def _mm_res_body(a_ref, b_ref, x_ref, o_ref):
    o_ref[...] = x_ref[...] + _dot(a_ref[...], b_ref[...])


def matmul_residual(a, w, x, tm=512, tn=1024):
    m, k = a.shape
    n = w.shape[1]
    return pl.pallas_call(
        _mm_res_body,
        grid=(n // tn, m // tm),
        in_specs=[pl.BlockSpec((tm, k), lambda j, i: (i, 0)), pl.BlockSpec((k, tn), lambda j, i: (0, j)),
                  pl.BlockSpec((tm, tn), lambda j, i: (i, j))],
        out_specs=pl.BlockSpec((tm, tn), lambda j, i: (i, j)),
        out_shape=jax.ShapeDtypeStruct((m, n), F32),
        compiler_params=_cparams(("parallel", "parallel")),
    )(a, w, x)


def _merge_body(ya_ref, yb_ref, yc_ref, yd_ref, wa_ref, wb_ref, wc_ref, wd_ref,
                ga_ref, gb_ref, gc_ref, gd_ref, o_ref):
    def gate(g_ref):
        return jax.nn.sigmoid(g_ref[...].astype(F32))

    acc = gate(ga_ref) * _dot(ya_ref[...], wa_ref[...])
    acc = acc + gate(gb_ref) * _dot(yb_ref[...], wb_ref[...])
    acc = acc + gate(gc_ref) * _dot(yc_ref[...], wc_ref[...])
    acc = acc + gate(gd_ref) * _dot(yd_ref[...], wd_ref[...])
    o_ref[...] = acc.astype(o_ref.dtype)


def gated_merge(ys, w_branch, proj, tm=512, tn=1024):
    m, kb = ys[0].shape
    d = w_branch.shape[2]
    gpb = d // tn
    g0 = CB_GATE0 * BRANCH_W // tn
    y_specs = [pl.BlockSpec((tm, kb), lambda j, i: (i, 0)) for _ in range(N_BRANCH)]
    w_specs = [pl.BlockSpec((None, kb, tn), functools.partial(lambda j, i, br: (br, 0, j), br=br))
               for br in range(N_BRANCH)]
    g_specs = [pl.BlockSpec((tm, tn), functools.partial(lambda j, i, br: (i, g0 + br * gpb + j), br=br))
               for br in range(N_BRANCH)]
    return pl.pallas_call(
        _merge_body,
        grid=(d // tn, m // tm),
        in_specs=y_specs + w_specs + g_specs,
        out_specs=pl.BlockSpec((tm, tn), lambda j, i: (i, j)),
        out_shape=jax.ShapeDtypeStruct((m, d), BF16),
        compiler_params=_cparams(("parallel", "parallel")),
    )(*ys, w_branch, w_branch, w_branch, w_branch, proj, proj, proj, proj)


def _sb_body(q_ref, k_ref, v_ref, o_ref, *, tq, tk, scale):
    qi = pl.program_id(2)
    q = (q_ref[...].astype(F32) * scale).astype(BF16)
    ur = lax.broadcasted_iota(jnp.int32, (tk, 2 * tk), 0)
    uc = lax.broadcasted_iota(jnp.int32, (tk, 2 * tk), 1)
    cum_w = jnp.where((uc >= tk) | (ur > uc), 1.0, 0.0).astype(BF16)

    def key_block(qq, start, nkeys, acc, c, mask):
        k = k_ref[pl.ds(start, nkeys), :]
        v = v_ref[pl.ds(start, nkeys), :]
        z = _dot_nt(qq, k)
        nz = -z
        lg = jnp.log(1.0 + jnp.exp(jnp.minimum(z, nz)))
        l1mb = jnp.minimum(nz, 0.0) - lg
        log_beta = l1mb + z
        if mask is not None:
            l1mb = jnp.where(mask, l1mb, 0.0)
        l1mb_b = l1mb.astype(BF16)
        r = nkeys // tk
        ws = [None] * r
        for j in range(r - 1, -1, -1):
            sl = slice(j * tk, (j + 1) * tk)
            t = _dot(l1mb_b[:, sl], cum_w)
            w = jnp.exp(log_beta[:, sl] + t[:, :tk] + c)
            if mask is not None:
                w = jnp.where(mask[:, sl], w, 0.0)
            ws[j] = w.astype(BF16)
            c = c + t[:, tk:]
        acc = acc + _dot(jnp.concatenate(ws, axis=1), v)
        return acc, c

    half = tq // 2
    base = pl.multiple_of(qi * tq, tq)
    row_t = lax.broadcasted_iota(jnp.int32, (half, half), 0)
    col_t = lax.broadcasted_iota(jnp.int32, (half, half), 1)
    acc_t, c_t = key_block(q[:half], base, half, jnp.zeros((half, HEAD_DIM), F32), jnp.zeros((half, tk), F32),
                           col_t < row_t)
    row_b = lax.broadcasted_iota(jnp.int32, (half, tq), 0) + half
    col_b = lax.broadcasted_iota(jnp.int32, (half, tq), 1)
    acc_b, c_b = key_block(q[half:], base, tq, jnp.zeros((half, HEAD_DIM), F32), jnp.zeros((half, tk), F32),
                           col_b < row_b)
    acc = jnp.concatenate([acc_t, acc_b], axis=0)
    c = jnp.concatenate([c_t, c_b], axis=0)

    def body(it, carry):
        return key_block(q, pl.multiple_of((qi - 1 - it) * tq, tq), tq, *carry, None)

    acc, c = lax.fori_loop(0, qi, body, (acc, c))
    o_ref[...] = acc.astype(o_ref.dtype)


def sb_attention(proj, batch, seq, tq=512, tk=128):
    m = proj.shape[0]
    nq = seq // tq
    return pl.pallas_call(
        functools.partial(_sb_body, tq=tq, tk=tk, scale=HEAD_DIM ** -0.5),
        grid=(batch, N_HEADS, nq),
        in_specs=[pl.BlockSpec((tq, HEAD_DIM), lambda b, h, i: (b * nq + i, CB_SB_Q * N_HEADS + h)),
                  pl.BlockSpec((seq, HEAD_DIM), lambda b, h, i: (b, CB_SB_K * N_HEADS + h)),
                  pl.BlockSpec((seq, HEAD_DIM), lambda b, h, i: (b, CB_SB_V * N_HEADS + h))],
        out_specs=pl.BlockSpec((tq, HEAD_DIM), lambda b, h, i: (b * nq + i, h)),
        out_shape=jax.ShapeDtypeStruct((m, BRANCH_W), BF16),
        compiler_params=_cparams(("parallel", "parallel", "arbitrary")),
    )(proj, proj, proj)


def _mlstm_body(q_ref, k_ref, v_ref, o_ref, g_ref, cw_ref, cb_ref, gb_ref, hn_ref, y_ref,
                xs_ref, c_ref, n_ref, m_ref, *, n_conv):
    L = CHUNK
    chunk = pl.program_id(1)

    @pl.when(chunk == 0)
    def _():
        xs_ref[0:8, :] = jnp.zeros((8, 2 * BRANCH_W), F32)
        c_ref[...] = jnp.zeros_like(c_ref)
        n_ref[...] = jnp.zeros_like(n_ref)
        m_ref[...] = jnp.zeros_like(m_ref)

    xs_ref[8:8 + L, 0:BRANCH_W] = q_ref[...].astype(F32)
    xs_ref[8:8 + L, BRANCH_W:2 * BRANCH_W] = k_ref[...].astype(F32)
    conv = jnp.zeros((L, 2 * BRANCH_W), F32) + cb_ref[...]
    for j in range(n_conv):
        conv = conv + cw_ref[j:j + 1, :] * xs_ref[pl.ds(8 - (n_conv - 1) + j, L), :]
    xs_ref[0:8, :] = xs_ref[L:L + 8, :]
    qk = conv * jax.nn.sigmoid(conv)
    qf = qk[:, 0:BRANCH_W]
    kf = qk[:, BRANCH_W:2 * BRANCH_W] * (HEAD_DIM ** -0.5)

    g = g_ref[...] + gb_ref[...]
    lf = -_softplus(-g)
    ri = lax.broadcasted_iota(jnp.int32, (L, L), 0)
    ci = lax.broadcasted_iota(jnp.int32, (L, L), 1)
    tril = ri >= ci
    tril_f = jnp.where(tril, 1.0, 0.0).astype(F32)
    triu_f = jnp.where(ri <= ci, 1.0, 0.0).astype(F32)
    b_cols = _dot_f32(tril_f, lf)
    g_t = g.T
    b_rows = _dot_f32(lf.T, triu_f)

    for h in range(N_HEADS):
        sl = slice(h * HEAD_DIM, (h + 1) * HEAD_DIM)
        qh = qf[:, sl]
        kh = kf[:, sl]
        qb = qh.astype(BF16)
        kb = kh.astype(BF16)
        vb = v_ref[:, sl]
        i_col = _lane_col(g, h)
        b_col = _lane_col(b_cols, N_HEADS + h)
        i_row = g_t[h:h + 1, :]
        b_row = b_rows[N_HEADS + h:N_HEADS + h + 1, :]
        m_prev = m_ref[h:h + 1, 0:1]
        c_prev = c_ref[h]
        n_prev = n_ref[h:h + 1, :]

        log_d = jnp.where(tril, b_col - b_row + i_row, -jnp.inf)
        m_inter = b_col + m_prev
        m_t = jnp.maximum(m_inter, jnp.max(log_d, axis=1, keepdims=True))
        dmat = jnp.exp(log_d - m_t)
        s = _dot_nt(qb, kb) * dmat
        inter = jnp.exp(m_inter - m_t)
        num = _dot(s.astype(BF16), vb) + inter * _dot(qb, c_prev.astype(BF16))
        den = jnp.sum(s, axis=1, keepdims=True) + inter * jnp.sum(qb.astype(F32) * n_prev, axis=1, keepdims=True)
        hh = num / jnp.maximum(jnp.abs(den), jnp.exp(-m_t))

        m_new = m_t[L - 1:L, :]
        b_last = b_col[L - 1:L, :]
        w_col = jnp.exp(b_last - b_col + i_col - m_new)
        decay = jnp.exp(b_last + m_prev - m_new)
        kw = kh * w_col
        c_ref[h] = decay * c_prev + _dot_tn(kw.astype(BF16), vb)
        n_ref[h:h + 1, :] = decay * n_prev + jnp.sum(kw, axis=0, keepdims=True)
        m_ref[h:h + 1, :] = jnp.broadcast_to(m_new, (1, LANES))

        hn = hh * lax.rsqrt(jnp.mean(hh * hh, axis=1, keepdims=True) + EPS) * hn_ref[:, sl]
        y_ref[:, sl] = (jax.nn.sigmoid(o_ref[:, sl].astype(F32)) * hn).astype(y_ref.dtype)


def mlstm(proj, tail, conv_w, conv_b, gate_b, head_norm, batch, seq):
    m = proj.shape[0]
    nc = seq // CHUNK
    n_conv = conv_w.shape[0]
    gb = jnp.zeros((1, LANES), F32).at[0, 0:2 * N_HEADS].set(gate_b.reshape(-1))

    def blk(cb):
        return pl.BlockSpec((CHUNK, BRANCH_W), lambda b, c: (b * nc + c, cb))

    def const(shape):
        return pl.BlockSpec(shape, lambda b, c: (0, 0))

    return pl.pallas_call(
        functools.partial(_mlstm_body, n_conv=n_conv),
        grid=(batch, nc),
        in_specs=[blk(CB_ML_Q), blk(CB_ML_K), blk(CB_ML_V), blk(CB_ML_O),
                  pl.BlockSpec((CHUNK, LANES), lambda b, c: (b * nc + c, 2)),
                  const((n_conv, 2 * BRANCH_W)), const((1, 2 * BRANCH_W)), const((1, LANES)),
                  const((1, BRANCH_W))],
        out_specs=pl.BlockSpec((CHUNK, BRANCH_W), lambda b, c: (b * nc + c, 0)),
        out_shape=jax.ShapeDtypeStruct((m, BRANCH_W), BF16),
        scratch_shapes=[pltpu.VMEM((CHUNK + 8, 2 * BRANCH_W), F32),
                        pltpu.VMEM((N_HEADS, HEAD_DIM, HEAD_DIM), F32),
                        pltpu.VMEM((N_HEADS, HEAD_DIM), F32),
                        pltpu.VMEM((N_HEADS, LANES), F32)],
        compiler_params=_cparams(("parallel", "arbitrary")),
    )(proj, proj, proj, proj, tail, conv_w, conv_b.reshape(1, -1), gb, head_norm.reshape(1, -1))


def _gelu(x):
    return 0.5 * x * (1.0 + lax.erf(x * (2.0 ** -0.5)))


def _sgu_body(u_ref, v_ref, w_ref, b_ref, o_ref):
    L = CHUNK
    u = _gelu(u_ref[...].astype(F32))
    v = _gelu(v_ref[...].astype(F32))
    mu = jnp.mean(v, axis=1, keepdims=True)
    vc = v - mu
    var = jnp.mean(vc * vc, axis=1, keepdims=True)
    vn = (vc * lax.rsqrt(var + EPS)).astype(BF16)
    ri = lax.broadcasted_iota(jnp.int32, (L, L), 0)
    ci = lax.broadcasted_iota(jnp.int32, (L, L), 1)
    tril = ri >= ci
    bias = b_ref[...]
    for g in range(N_HEADS):
        sl = slice(g * HEAD_DIM, (g + 1) * HEAD_DIM)
        wg = jnp.where(tril, w_ref[g], 0.0).astype(BF16)
        mixed = _dot(wg, vn[:, sl]) + _lane_col(bias, g)
        o_ref[:, sl] = (u[:, sl] * mixed).astype(o_ref.dtype)


def spatial_gating(proj, sg_w, sg_b):
    m = proj.shape[0]
    bias = jnp.zeros((CHUNK, LANES), F32).at[:, 0:N_HEADS].set(sg_b.T)
    return pl.pallas_call(
        _sgu_body,
        grid=(m // CHUNK,),
        in_specs=[pl.BlockSpec((CHUNK, BRANCH_W), lambda i: (i, CB_SG_U)),
                  pl.BlockSpec((CHUNK, BRANCH_W), lambda i: (i, CB_SG_V)),
                  pl.BlockSpec((N_HEADS, CHUNK, CHUNK), lambda i: (0, 0, 0)),
                  pl.BlockSpec((CHUNK, LANES), lambda i: (0, 0))],
        out_specs=pl.BlockSpec((CHUNK, BRANCH_W), lambda i: (i, 0)),
        out_shape=jax.ShapeDtypeStruct((m, BRANCH_W), BF16),
        compiler_params=_cparams(("parallel",)),
    )(proj, proj, sg_w, bias)


def _swa_body(q_ref, kp_ref, kc_ref, vp_ref, vc_ref, qg_ref, kg_ref, sink_ref, o_ref, *, n_chunks):
    W = CHUNK
    n = pl.program_id(0) % n_chunks
    lane = lax.broadcasted_iota(jnp.int32, (1, LANES), 1)
    lo = lane < SW_DIM

    def pair_norm(x, gain):
        x2 = x * x
        s_lo = jnp.sum(jnp.where(lo, x2, 0.0), axis=1, keepdims=True)
        s_hi = jnp.sum(jnp.where(lo, 0.0, x2), axis=1, keepdims=True)
        ms = jnp.where(lo, s_lo, s_hi) * (1.0 / SW_DIM)
        return x * lax.rsqrt(ms + EPS) * gain

    kg = kg_ref[...]
    kk = jnp.concatenate([pair_norm(kp_ref[...], kg), pair_norm(kc_ref[...], kg)], axis=0)
    vv = jnp.concatenate([vp_ref[...], vc_ref[...]], axis=0)
    kk_sw = pltpu.roll(kk, SW_DIM, axis=1)
    vv_sw = pltpu.roll(vv, SW_DIM, axis=1)
    k_ver = [[jnp.where(lo, kk, 0.0).astype(BF16), jnp.where(lo, 0.0, kk_sw).astype(BF16)],
             [jnp.where(lo, kk_sw, 0.0).astype(BF16), jnp.where(lo, 0.0, kk).astype(BF16)]]
    v_ver = [[jnp.where(lo, vv, 0.0).astype(BF16), jnp.where(lo, 0.0, vv_sw).astype(BF16)],
             [jnp.where(lo, vv_sw, 0.0).astype(BF16), jnp.where(lo, 0.0, vv).astype(BF16)]]

    stack = SW_QHEADS // SW_KVHEADS // 2
    qi = lax.broadcasted_iota(jnp.int32, (stack * W, 2 * W), 0) % W
    kj = lax.broadcasted_iota(jnp.int32, (stack * W, 2 * W), 1)
    diff = W + qi - kj
    valid = (diff >= 0) & (diff < W) & ((n > 0) | (kj >= W))
    scale = SW_DIM ** -0.5
    qg = qg_ref[...]
    qn = [pair_norm(q_ref[:, pb * LANES:(pb + 1) * LANES].astype(F32), qg) for pb in range(SW_QHEADS // 2)]
    for hk in range(SW_KVHEADS):
        outs = []
        for half in range(2):
            sel = lo if half == 0 else jnp.logical_not(lo)
            qs = jnp.concatenate([jnp.where(sel, qn[hk * stack + i], 0.0) for i in range(stack)], axis=0)
            s = jnp.where(valid, _dot_nt(qs.astype(BF16), k_ver[hk][half]) * scale, -jnp.inf)
            st = hk * 2 + half
            sink = sink_ref[st * stack * W:(st + 1) * stack * W, :]
            mx = jnp.maximum(jnp.max(s, axis=1, keepdims=True), sink)
            p = jnp.exp(s - mx)
            inv = 1.0 / (jnp.sum(p, axis=1, keepdims=True) + jnp.exp(sink - mx))
            outs.append(_dot(p.astype(BF16), v_ver[hk][half]) * inv)
        for i in range(stack):
            pb = hk * stack + i
            o_ref[:, pb * LANES:(pb + 1) * LANES] = (outs[0][i * W:(i + 1) * W]
                                                     + outs[1][i * W:(i + 1) * W]).astype(o_ref.dtype)


def swa_attention(proj, tail, q_norm, k_norm, sinks, seq):
    m = proj.shape[0]
    nc = seq // CHUNK
    qg = jnp.tile(q_norm.reshape(1, SW_DIM), (1, LANES // SW_DIM))
    kg = jnp.tile(k_norm.reshape(1, SW_DIM), (1, LANES // SW_DIM))
    stack = SW_QHEADS // SW_KVHEADS // 2
    order = [2 * (hk * stack + i) + half for hk in range(SW_KVHEADS) for half in range(2) for i in range(stack)]
    sink_rows = jnp.repeat(sinks.astype(F32)[jnp.array(order)], CHUNK).reshape(SW_QHEADS * CHUNK, 1)

    def prev(i):
        return jnp.maximum(i - 1, 0)

    return pl.pallas_call(
        functools.partial(_swa_body, n_chunks=nc),
        grid=(m // CHUNK,),
        in_specs=[pl.BlockSpec((CHUNK, BRANCH_W), lambda i: (i, CB_SW_Q)),
                  pl.BlockSpec((CHUNK, LANES), lambda i: (prev(i), 0)),
                  pl.BlockSpec((CHUNK, LANES), lambda i: (i, 0)),
                  pl.BlockSpec((CHUNK, LANES), lambda i: (prev(i), 1)),
                  pl.BlockSpec((CHUNK, LANES), lambda i: (i, 1)),
                  pl.BlockSpec((1, LANES), lambda i: (0, 0)),
                  pl.BlockSpec((1, LANES), lambda i: (0, 0)),
                  pl.BlockSpec((SW_QHEADS * CHUNK, 1), lambda i: (0, 0))],
        out_specs=pl.BlockSpec((CHUNK, BRANCH_W), lambda i: (i, 0)),
        out_shape=jax.ShapeDtypeStruct((m, BRANCH_W), BF16),
        compiler_params=_cparams(("parallel",)),
    )(proj, tail, tail, tail, tail, qg, kg, sink_rows)


def _ffn_gu_body(a_ref, bg_ref, bu_ref, o_ref):
    a = a_ref[...]
    g = _dot(a, bg_ref[...])
    u = _dot(a, bu_ref[...])
    o_ref[...] = (g * jax.nn.sigmoid(g) * u).astype(o_ref.dtype)


def ffn_gate_up(h, w_gu, tm=1024, tn=512, sides=()):
    m, k = h.shape
    f = w_gu.shape[1] // 2
    nj = f // tn
    (act,), side_outs = _hosted_call(
        _ffn_gu_body, (nj, m // tm),
        [pl.BlockSpec((tm, k), lambda j, i: (i, 0)),
         pl.BlockSpec((k, tn), lambda j, i: (0, j)),
         pl.BlockSpec((k, tn), lambda j, i: (0, nj + j))],
        [pl.BlockSpec((tm, tn), lambda j, i: (i, j))],
        [jax.ShapeDtypeStruct((m, f), BF16)], (h, w_gu, w_gu), sides)
    return act, side_outs


def _ffn_down_body(a_ref, b_ref, x_ref, o_ref):
    kk = pl.program_id(2)
    part = _dot(a_ref[...], b_ref[...])

    @pl.when(kk == 0)
    def _():
        o_ref[...] = x_ref[...] + part

    @pl.when(kk > 0)
    def _():
        o_ref[...] += part


def ffn_down(act, w_down, x, tm=1024, tn=1024, tk=3584):
    m, f = act.shape
    n = w_down.shape[1]
    nk = f // tk
    return pl.pallas_call(
        _ffn_down_body,
        grid=(n // tn, m // tm, nk),
        in_specs=[pl.BlockSpec((tm, tk), lambda j, i, k: (i, k)),
                  pl.BlockSpec((tk, tn), lambda j, i, k: (k, j)),
                  pl.BlockSpec((tm, tn), lambda j, i, k: (i, j))],
        out_specs=pl.BlockSpec((tm, tn), lambda j, i, k: (i, j)),
        out_shape=jax.ShapeDtypeStruct((m, n), F32),
        compiler_params=_cparams(("parallel", "parallel", "arbitrary")),
    )(act, w_down, x)


MOE_TM = 512
ROW_SLABS = D_MODEL // LANES


def _router_body(x_ref, g_ref, r_ref, h_ref, idx_ref, wts_ref):
    x = x_ref[...]
    ms = jnp.mean(x * x, axis=-1, keepdims=True)
    h = x * lax.rsqrt(ms + EPS) * g_ref[...]
    h_ref[...] = h.astype(h_ref.dtype)
    logits = _dot_f32(h, r_ref[...])
    lane = lax.broadcasted_iota(jnp.int32, logits.shape, 1)
    logits = jnp.where(lane < N_EXPERTS, logits, -jnp.inf)
    m1 = jnp.max(logits, axis=1, keepdims=True)
    i1 = jnp.min(jnp.where(logits == m1, lane, LANES), axis=1, keepdims=True)
    rest = jnp.where(lane == i1, -jnp.inf, logits)
    m2 = jnp.max(rest, axis=1, keepdims=True)
    i2 = jnp.min(jnp.where(rest == m2, lane, LANES), axis=1, keepdims=True)
    e2 = jnp.exp(m2 - m1)
    w1 = 1.0 / (1.0 + e2)
    w2 = e2 / (1.0 + e2)
    idx_ref[...] = jnp.where(lane == 0, i1, jnp.where(lane == 1, i2, 0))
    wts_ref[...] = jnp.where(lane == 0, w1, jnp.where(lane == 1, w2, 0.0))


def norm_and_route(x, g, router, tm=256):
    m, d = x.shape
    r_pad = jnp.zeros((d, LANES), F32).at[:, 0:N_EXPERTS].set(router)
    return pl.pallas_call(
        _router_body,
        grid=(m // tm,),
        in_specs=[pl.BlockSpec((tm, d), lambda i: (i, 0)), pl.BlockSpec((1, d), lambda i: (0, 0)),
                  pl.BlockSpec((d, LANES), lambda i: (0, 0))],
        out_specs=[pl.BlockSpec((tm, d), lambda i: (i, 0)), pl.BlockSpec((tm, LANES), lambda i: (i, 0)),
                   pl.BlockSpec((tm, LANES), lambda i: (i, 0))],
        out_shape=[jax.ShapeDtypeStruct((m, d), BF16), jax.ShapeDtypeStruct((m, LANES), jnp.int32),
                   jax.ShapeDtypeStruct((m, LANES), F32)],
        compiler_params=_cparams(("parallel",)),
    )(x, g.reshape(1, d), r_pad)


def _route_plan(idx, tm):
    m = idx.shape[0]
    n_tiles_max = -(-(2 * m + N_EXPERTS * (tm - 1)) // tm)
    e_flat = jnp.concatenate([idx[:, 0], idx[:, 1]])
    onehot = (e_flat[:, None] == jnp.arange(N_EXPERTS, dtype=jnp.int32)[None, :]).astype(jnp.int32)
    csum = jnp.cumsum(onehot, axis=0)
    counts = csum[-1]
    rank = jnp.sum((csum - onehot) * onehot, axis=1)
    padded = (counts + tm - 1) // tm * tm
    ends = jnp.cumsum(padded)
    starts = ends - padded
    dest = (jnp.sum(onehot * starts[None, :], axis=1) + rank).astype(jnp.int32)
    tile_row0 = jnp.arange(n_tiles_max, dtype=jnp.int32) * tm
    tile_expert = jnp.minimum(jnp.sum((tile_row0[:, None] >= ends[None, :]).astype(jnp.int32), axis=1),
                              N_EXPERTS - 1).astype(jnp.int32)
    n_tiles = (ends[-1] // tm).astype(jnp.int32).reshape(1)
    token = jnp.arange(2 * m, dtype=jnp.int32) % m
    row_token = jnp.zeros((n_tiles_max * tm,), jnp.int32).at[dest].set(token, unique_indices=True)
    return dest, tile_expert, n_tiles, row_token


def _gather_rows_body(idx_ref, src_ref, o_ref, sem, *, tile):
    base = pl.program_id(0) * tile

    def start(r, carry):
        pltpu.make_async_copy(src_ref.at[idx_ref[base + r]], o_ref.at[r], sem).start()
        return carry

    lax.fori_loop(0, tile, start, 0, unroll=8)
    pltpu.make_async_copy(src_ref.at[pl.ds(0, tile)], o_ref, sem).wait()


def gather_rows(src3, idx, tile=1024):
    n = idx.shape[0]
    slab = src3.shape[1:]
    return pl.pallas_call(
        functools.partial(_gather_rows_body, tile=tile),
        grid_spec=pltpu.PrefetchScalarGridSpec(
            num_scalar_prefetch=1, grid=(n // tile,),
            in_specs=[pl.BlockSpec(memory_space=pl.ANY)],
            out_specs=pl.BlockSpec((tile,) + slab, lambda i, idx_ref: (i, 0, 0)),
            scratch_shapes=[pltpu.SemaphoreType.DMA(())]),
        out_shape=jax.ShapeDtypeStruct((n,) + slab, src3.dtype),
        compiler_params=_cparams(("arbitrary",)),
    )(idx, src3)


def _live_tile(t, nt_ref):
    return jnp.minimum(t, nt_ref[0] - 1)


def _moe_gu_body(te_ref, nt_ref, a_ref, bg_ref, bu_ref, o_ref):
    t = pl.program_id(1)

    @pl.when(t < nt_ref[0])
    def _():
        a = a_ref[...]
        g = _dot(a, bg_ref[...])
        u = _dot(a, bu_ref[...])
        o_ref[...] = (g * jax.nn.sigmoid(g) * u).astype(o_ref.dtype)

    @pl.when(t >= nt_ref[0])
    def _():
        o_ref[...] = jnp.zeros_like(o_ref)


def moe_gate_up(xs, w_gu, tile_expert, n_tiles, tm, tn=512):
    rows, k = xs.shape
    f = w_gu.shape[2] // 2
    nj = f // tn
    return pl.pallas_call(
        _moe_gu_body,
        grid_spec=pltpu.PrefetchScalarGridSpec(
            num_scalar_prefetch=2, grid=(nj, rows // tm),
            in_specs=[pl.BlockSpec((tm, k), lambda j, t, te, nt: (_live_tile(t, nt), 0)),
                      pl.BlockSpec((None, k, tn), lambda j, t, te, nt: (te[_live_tile(t, nt)], 0, j)),
                      pl.BlockSpec((None, k, tn), lambda j, t, te, nt: (te[_live_tile(t, nt)], 0, nj + j))],
            out_specs=pl.BlockSpec((tm, tn), lambda j, t, te, nt: (t, j))),
        out_shape=jax.ShapeDtypeStruct((rows, f), BF16),
        compiler_params=_cparams(("arbitrary", "arbitrary")),
    )(tile_expert, n_tiles, xs, w_gu, w_gu)


def _moe_down_body(te_ref, nt_ref, a_ref, b_ref, o_ref):
    t = pl.program_id(1)

    @pl.when(t < nt_ref[0])
    def _():
        o_ref[...] = _dot(a_ref[...], b_ref[...]).astype(o_ref.dtype)

    @pl.when(t >= nt_ref[0])
    def _():
        o_ref[...] = jnp.zeros_like(o_ref)


def moe_down(act, w_down, tile_expert, n_tiles, tm, tn=1024):
    rows, f = act.shape
    n = w_down.shape[2]
    return pl.pallas_call(
        _moe_down_body,
        grid_spec=pltpu.PrefetchScalarGridSpec(
            num_scalar_prefetch=2, grid=(n // tn, rows // tm),
            in_specs=[pl.BlockSpec((tm, f), lambda j, t, te, nt: (_live_tile(t, nt), 0)),
                      pl.BlockSpec((None, f, tn), lambda j, t, te, nt: (te[_live_tile(t, nt)], 0, j))],
            out_specs=pl.BlockSpec((tm, tn), lambda j, t, te, nt: (t, j))),
        out_shape=jax.ShapeDtypeStruct((rows, n), BF16),
        compiler_params=_cparams(("arbitrary", "arbitrary")),
    )(tile_expert, n_tiles, act, w_down)


def _combine_body(x_ref, y1_ref, y2_ref, w_ref, o_ref):
    w = w_ref[...]
    o_ref[...] = (x_ref[...] + _lane_col(w, 0) * y1_ref[...].astype(F32)
                  + _lane_col(w, 1) * y2_ref[...].astype(F32))


def moe_combine(x, yg, wts, tm=256):
    m, d = x.shape
    nb = m // tm
    return pl.pallas_call(
        _combine_body,
        grid=(nb,),
        in_specs=[pl.BlockSpec((tm, d), lambda i: (i, 0)), pl.BlockSpec((tm, d), lambda i: (i, 0)),
                  pl.BlockSpec((tm, d), lambda i: (nb + i, 0)), pl.BlockSpec((tm, LANES), lambda i: (i, 0))],
        out_specs=pl.BlockSpec((tm, d), lambda i: (i, 0)),
        out_shape=jax.ShapeDtypeStruct((m, d), F32),
        compiler_params=_cparams(("parallel",)),
    )(x, yg, yg, wts)


def moe_ffn(x, g, router, w_gu, w_down, tm=MOE_TM):
    m, d = x.shape
    h, idx, wts = norm_and_route(x, g, router)
    dest, tile_expert, n_tiles, row_token = _route_plan(idx, tm)
    rows = row_token.shape[0]
    xs3 = gather_rows(h.reshape(m, ROW_SLABS, d // ROW_SLABS), row_token)
    act = moe_gate_up(xs3.reshape(rows, d), w_gu, tile_expert, n_tiles, tm)
    ys = moe_down(act, w_down, tile_expert, n_tiles, tm)
    yg3 = gather_rows(ys.reshape(rows, ROW_SLABS, d // ROW_SLABS), dest)
    return moe_combine(x, yg3.reshape(2 * m, d), wts)


def _w_in_pieces():
    a_w = BRANCH_W
    o_if = 3 * a_w + 4 * a_w
    o_sg = o_if + 2 * N_HEADS
    o_swk = o_sg + 2 * a_w + SW_QHEADS * SW_DIM
    o_gate = o_swk + 2 * SW_KVHEADS * SW_DIM
    n_in = o_gate + N_BRANCH * D_MODEL
    main = ((0, o_if), (o_sg, o_swk), (o_gate, n_in))
    tail = ((o_swk, o_gate), (o_if, o_sg))
    return main, tail


def _w_in_cast(w_in_all, layer, block_rows=16):
    main_p, tail_p = _w_in_pieces()
    return SideCast(w_in_all, block_rows, ((N_MAIN, main_p), (N_TAIL, tail_p)), layer)


def _branch_casts(w_branch, w_out, layer):
    n_layers, nb, kb, d = w_branch.shape
    return (_plain_cast(w_branch.reshape(n_layers, nb * kb, d), 16, layer), _plain_cast(w_out, 16, layer))


def _mixer(h, w_main, w_tail, layer_params, batch, seq, sides=(), tm=1024):
    (conv_w, conv_b, gate_b, head_norm, sg_w, sg_b, q_norm, k_norm, sinks) = layer_params
    proj, side_outs = inproj_main(h, w_main, tm=tm, sides=sides)
    tail = inproj_tail(h, w_tail)
    y_a = sb_attention(proj, batch, seq)
    y_b = mlstm(proj, tail, conv_w, conv_b, gate_b, head_norm, batch, seq)
    y_c = spatial_gating(proj, sg_w, sg_b)
    y_d = swa_attention(proj, tail, q_norm, k_norm, sinks, seq)
    return proj, (y_a, y_b, y_c, y_d), side_outs


def kernel(x, norm_mix, w_in, ml_conv_w, ml_conv_b, ml_gate_b, ml_head_norm, sg_w, sg_b, sw_q_norm, sw_k_norm,
           sw_sinks, w_branch, w_out, norm_ffn, ffn_w_gu, ffn_w_down, moe_router, moe_w_gu, moe_w_down):
    batch, seq, d = x.shape
    depth = norm_mix.shape[0]
    nb, kb = w_branch.shape[1:3]
    xf = x.reshape(batch * seq, d)
    w_main, w_tail = cast_weights((_w_in_cast(w_in, 0, 64),))
    w_br_b = w_o_b = moe_down_b = None
    for layer in range(depth):
        j = layer // 2
        dense = layer % 2 == 0
        has_next = layer + 1 < depth
        params = (ml_conv_w[layer], ml_conv_b[layer], ml_gate_b[layer], ml_head_norm[layer], sg_w[layer],
                  sg_b[layer], sw_q_norm[layer], sw_k_norm[layer], sw_sinks[layer])
        h = rmsnorm(xf, norm_mix[layer])
        if dense:
            sides = (_plain_cast(ffn_w_gu, 16, j), _plain_cast(ffn_w_down, 64, j)) + _branch_casts(w_branch, w_out, layer)
            proj, ys, (w_gu_b, w_down_b, w_br_b, w_o_b) = _mixer(h, w_main, w_tail, params, batch, seq, sides)
        else:
            n_moe, e, k, f2 = moe_w_gu.shape
            sides = (_plain_cast(moe_w_gu.reshape(n_moe, e * k, f2), 64, j),)
            proj, ys, (moe_gu_b,) = _mixer(h, w_main, w_tail, params, batch, seq, sides, tm=512)
            moe_gu_b = moe_gu_b.reshape(e, k, f2)
            if w_br_b is None:
                w_br_b, w_o_b = cast_weights(_branch_casts(w_branch, w_out, layer))
        merged = gated_merge(ys, w_br_b.reshape(nb, kb, d), proj)
        xf = matmul_residual(merged, w_o_b, xf)
        w_br_b = w_o_b = None
        if dense:
            h = rmsnorm(xf, norm_ffn[layer])
            if has_next:
                n_moe, e, f, n = moe_w_down.shape
                sides = ((_w_in_cast(w_in, layer + 1), _plain_cast(moe_w_down.reshape(n_moe, e * f, n), 64, j))
                         + _branch_casts(w_branch, w_out, layer + 1))
                act, (w_main, w_tail, moe_down_b, w_br_b, w_o_b) = ffn_gate_up(h, w_gu_b, sides=sides)
                moe_down_b = moe_down_b.reshape(e, f, n)
            else:
                act, _ = ffn_gate_up(h, w_gu_b)
            xf = ffn_down(act, w_down_b, xf)
        else:
            if moe_down_b is None:
                moe_down_b = moe_w_down[j].astype(BF16)
            xf = moe_ffn(xf, norm_ffn[layer], moe_router[j], moe_gu_b, moe_down_b)
            moe_down_b = None
            if has_next:
                w_main, w_tail = cast_weights((_w_in_cast(w_in, layer + 1, 64),))
    return xf.reshape(batch, seq, d)
```
